```python
import jax, jax.numpy as jnp
from jax import lax
import numpy as np

D_MODEL = 1024
BATCH = 2
SEQ = 8192
DEPTH = 4

HEAD_DIM = 64
N_BRANCH = 4
BRANCH_WIDTH = D_MODEL // N_BRANCH
N_META = 16
BLOCK = 128
N_FRONT_PAD = BLOCK - N_META
SB_HEADS = BRANCH_WIDTH // HEAD_DIM
GLA_HEADS = BRANCH_WIDTH // HEAD_DIM
GLA_DK = HEAD_DIM // 2
GLA_DV = HEAD_DIM
GLA_LOW_RANK = 16
GLA_GATE_NORMALIZER = 16.0
ML_HEADS = BRANCH_WIDTH // HEAD_DIM
ML_DH = HEAD_DIM
CONV_WIDTH = 4
RET_HEADS = BRANCH_WIDTH // HEAD_DIM
RET_DH = HEAD_DIM
D_FF = -(-8 * D_MODEL // (3 * 256)) * 256
ROPE_BASE = 10000.0
NORM_EPS = 1e-6

IN_SPLIT_SIZES = (
    BRANCH_WIDTH, BRANCH_WIDTH, BRANCH_WIDTH,
    GLA_HEADS * GLA_DK, GLA_HEADS * GLA_DK, GLA_HEADS * GLA_DV,
    GLA_HEADS * GLA_DV, GLA_LOW_RANK,
    2 * BRANCH_WIDTH, BRANCH_WIDTH, ML_HEADS, ML_HEADS, BRANCH_WIDTH,
    BRANCH_WIDTH, BRANCH_WIDTH, BRANCH_WIDTH, BRANCH_WIDTH,
    N_BRANCH * D_MODEL,
)
IN_WIDTH = sum(IN_SPLIT_SIZES)

kernel_name = "hybrid_sb_gla_mlstm_retnet_block"

F32 = jnp.float32


def _rmsnorm(x, g):
    xf = x.astype(F32)
    y = xf * lax.rsqrt(jnp.mean(xf * xf, axis=-1, keepdims=True) + NORM_EPS)
    return (y * g.astype(F32)).astype(x.dtype)


def _head_rmsnorm(y, g, n_heads):
    b, l, w = y.shape
    yh = y.reshape(b, l, n_heads, w // n_heads)
    yh = yh * lax.rsqrt(jnp.mean(yh * yh, axis=-1, keepdims=True) + NORM_EPS)
    return yh.reshape(b, l, w) * g.astype(F32)


def _to_chunks(t, n_heads):
    b, l, w = t.shape
    return t.reshape(b, l // BLOCK, BLOCK, n_heads, w // n_heads).transpose(0, 3, 1, 2, 4)


def _gate_chunks(t):
    b, l, h = t.shape
    return t.reshape(b, l // BLOCK, BLOCK, h).transpose(0, 3, 1, 2)


def _from_chunks(t):
    b, h, nc, c, d = t.shape
    return t.transpose(0, 2, 3, 1, 4).reshape(b, nc * c, h * d)


def _causal_tril():
    return jnp.tril(jnp.ones((BLOCK, BLOCK), dtype=bool))


def _scan_chunk_states(decay, local):
    decay = jnp.broadcast_to(decay, local.shape)

    def step(state, inp):
        d, u = inp
        return d * state + u, state

    _, prev = lax.scan(step, jnp.zeros_like(local[:, :, 0]),
                       (jnp.moveaxis(decay, 2, 0), jnp.moveaxis(local, 2, 0)))
    return jnp.moveaxis(prev, 0, 2)


def _causal_conv(x, w):
    ch = x.shape[-1]
    return lax.conv_general_dilated(x, w[:, None, :].astype(x.dtype), window_strides=(1,),
                                    padding=[(CONV_WIDTH - 1, 0)],
                                    dimension_numbers=('NWC', 'WIO', 'NWC'),
                                    feature_group_count=ch)


def _rotary(x, positions, n_heads):
    b, l, w = x.shape
    d = w // n_heads
    half = d // 2
    inv_freq = ROPE_BASE ** (-jnp.arange(half, dtype=F32) / half)
    ang = positions.astype(F32)[:, None] * inv_freq[None, :]
    cos, sin = jnp.cos(ang)[:, None, :], jnp.sin(ang)[:, None, :]
    xh = x.reshape(b, l, n_heads, d)
    x1, x2 = xh[..., :half], xh[..., half:]
    return jnp.concatenate([x1 * cos - x2 * sin, x1 * sin + x2 * cos], axis=-1).reshape(b, l, w)


def _stick_breaking(q, k, v, key_valid):
    b, l, w = q.shape
    nb = l // BLOCK
    qh = q.reshape(b, nb, BLOCK, SB_HEADS, HEAD_DIM).transpose(1, 0, 3, 2, 4)
    kh = k.reshape(b, l, SB_HEADS, HEAD_DIM).transpose(0, 2, 1, 3)
    vh = v.reshape(b, l, SB_HEADS, HEAD_DIM).transpose(0, 2, 1, 3)
    key_pos = jnp.arange(l)
    scale = HEAD_DIM ** -0.5

    def one_block(args):
        q_blk, blk = args
        q_pos = blk * BLOCK + jnp.arange(BLOCK)
        z = jnp.einsum('bhqd,bhkd->bhqk', q_blk, kh) * scale
        mask = (key_pos[None, :] < q_pos[:, None]) & key_valid[None, :]
        log_beta = jnp.where(mask, jax.nn.log_sigmoid(z), -jnp.inf)
        log_keep = jnp.where(mask, jax.nn.log_sigmoid(-z), 0.0)
        later = lax.cumsum(log_keep, axis=3, reverse=True) - log_keep
        a = jnp.exp(log_beta + later)
        return jnp.einsum('bhqk,bhkd->bhqd', a, vh)

    out = lax.map(one_block, (qh, jnp.arange(nb)))
    return out.transpose(1, 0, 3, 2, 4).reshape(b, l, w)


def _gla(q, k, v, r, code, w_gate_up, b_gate, norm_g):
    log_a = jax.nn.log_sigmoid(code @ w_gate_up.astype(F32) + b_gate.astype(F32)) / GLA_GATE_NORMALIZER
    qc = _to_chunks(q, GLA_HEADS) * GLA_DK ** -0.5
    kc = _to_chunks(k, GLA_HEADS)
    vc = _to_chunks(v, GLA_HEADS)
    cum = jnp.cumsum(_to_chunks(log_a, GLA_HEADS), axis=3)
    cum_last = cum[:, :, :, -1:, :]
    q_dec = qc * jnp.exp(cum)
    k_dec = kc * jnp.exp(-cum)
    scores = jnp.where(_causal_tril(), jnp.einsum('bhnqd,bhnkd->bhnqk', q_dec, k_dec), 0.0)
    intra = jnp.einsum('bhnqk,bhnke->bhnqe', scores, vc)
    local = jnp.einsum('bhnkd,bhnke->bhnde', kc * jnp.exp(cum_last - cum), vc)
    states = _scan_chunk_states(jnp.swapaxes(jnp.exp(cum_last), -1, -2), local)
    inter = jnp.einsum('bhnqd,bhnde->bhnqe', q_dec, states)
    o = _from_chunks(intra + inter)
    return _head_rmsnorm(o, norm_g, GLA_HEADS) * jax.nn.silu(r)


def _mlstm(qk, v, i_pre, f_pre, o_pre, conv_w, b_i, b_f, norm_g):
    qk = jax.nn.silu(_causal_conv(qk, conv_w))
    q, k = jnp.split(qk, 2, axis=-1)
    qc = _to_chunks(q, ML_HEADS)
    kc = _to_chunks(k, ML_HEADS) * ML_DH ** -0.5
    vc = _to_chunks(v, ML_HEADS)
    log_i = _gate_chunks(i_pre + b_i.astype(F32))
    log_f = jax.nn.log_sigmoid(_gate_chunks(f_pre + b_f.astype(F32)))
    cum_f = jnp.cumsum(log_f, axis=-1)
    cum_last = cum_f[..., -1]
    w_end = cum_last[..., None] - cum_f + log_i
    a_end = jnp.max(w_end, axis=-1)
    p_end = jnp.exp(w_end - a_end[..., None])
    c_loc = jnp.einsum('bhnk,bhnkd,bhnke->bhnde', p_end, kc, vc)
    n_loc = jnp.einsum('bhnk,bhnkd->bhnd', p_end, kc)

    def step(carry, inp):
        c_s, n_s, m_s = carry
        f_l, a_l, c_l, n_l = inp
        m_new = jnp.maximum(f_l + m_s, a_l)
        s_prev = jnp.exp(f_l + m_s - m_new)
        s_loc = jnp.exp(a_l - m_new)
        c_new = s_prev[..., None, None] * c_s + s_loc[..., None, None] * c_l
        n_new = s_prev[..., None] * n_s + s_loc[..., None] * n_l
        return (c_new, n_new, m_new), (c_s, n_s, m_s)

    b, h, nc, c, d = qc.shape
    init = (jnp.zeros((b, h, d, d), F32), jnp.zeros((b, h, d), F32), jnp.zeros((b, h), F32))
    xs = tuple(jnp.moveaxis(t, 2, 0) for t in (cum_last, a_end, c_loc, n_loc))
    _, (c_prev, n_prev, m_prev) = lax.scan(step, init, xs)
    c_prev, n_prev, m_prev = (jnp.moveaxis(t, 0, 2) for t in (c_prev, n_prev, m_prev))
    inter_log = cum_f + m_prev[..., None]
    intra_log = jnp.where(_causal_tril(),
                          cum_f[..., :, None] - cum_f[..., None, :] + log_i[..., None, :], -jnp.inf)
    m_t = jnp.maximum(inter_log, jnp.max(intra_log, axis=-1))
    w_intra = jnp.exp(intra_log - m_t[..., None])
    w_inter = jnp.exp(inter_log - m_t)
    s = jnp.einsum('bhnqd,bhnkd->bhnqk', qc, kc) * w_intra
    num = jnp.einsum('bhnqk,bhnke->bhnqe', s, vc) + w_inter[..., None] * jnp.einsum('bhnqd,bhnde->bhnqe', qc, c_prev)
    den = jnp.sum(s, axis=-1) + w_inter * jnp.einsum('bhnqd,bhnd->bhnq', qc, n_prev)
    h_cell = num / jnp.maximum(jnp.abs(den), jnp.exp(-m_t))[..., None]
    return jax.nn.sigmoid(o_pre) * _head_rmsnorm(_from_chunks(h_cell), norm_g, ML_HEADS)


def _retention(q, k, v, g, positions, norm_g):
    log_gamma = jnp.log1p(-jnp.exp2(-5.0 - jnp.arange(RET_HEADS, dtype=F32)))
    qc = _to_chunks(_rotary(q, positions, RET_HEADS), RET_HEADS)
    kc = _to_chunks(_rotary(k, positions, RET_HEADS), RET_HEADS) * RET_DH ** -0.5
    vc = _to_chunks(v, RET_HEADS)
    pos = jnp.arange(BLOCK, dtype=F32)
    diff = jnp.maximum(pos[:, None] - pos[None, :], 0.0)
    decay_intra = jnp.where(_causal_tril(), jnp.exp(log_gamma[:, None, None] * diff), 0.0)
    scores = jnp.einsum('bhnqd,bhnkd->bhnqk', qc, kc) * decay_intra[None, :, None]
    intra = jnp.einsum('bhnqk,bhnke->bhnqe', scores, vc)
    q_decay = jnp.exp(log_gamma[:, None] * (pos + 1.0))
    k_decay = jnp.exp(log_gamma[:, None] * (BLOCK - 1.0 - pos))
    local = jnp.einsum('bhnkd,bhnke->bhnde', kc * k_decay[None, :, None, :, None], vc)
    states = _scan_chunk_states(jnp.exp(log_gamma * BLOCK)[None, :, None, None, None], local)
    inter = jnp.einsum('bhnqd,bhnde->bhnqe', qc * q_decay[None, :, None, :, None], states)
    y = _from_chunks(intra + inter)
    return jax.nn.silu(g) * _head_rmsnorm(y, norm_g, RET_HEADS)


def _hybrid_layer(h, valid, positions, g_mix_pre, g_mix_post, g_ffn_pre, g_ffn_post, w_in,
                  gla_w_gate_up, gla_b_gate, gla_norm, ml_conv, ml_b_i, ml_b_f, ml_norm, ret_norm,
                  w_branch, b_merge, w_out, ffn_w_gate, ffn_w_up, ffn_w_down):
    b, l, _ = h.shape
    hn = _rmsnorm(h, g_mix_pre) * valid[None, :, None].astype(h.dtype)
    u = (hn @ w_in).astype(F32)
    split_points = [int(p) for p in np.cumsum(IN_SPLIT_SIZES)[:-1]]
    (sb_q, sb_k, sb_v, gla_q, gla_k, gla_v, gla_r, gla_code,
     ml_qk, ml_v, ml_i, ml_f, ml_o, ret_q, ret_k, ret_v, ret_g, merge_logits) = jnp.split(u, split_points, axis=-1)
    y_sb = _stick_breaking(sb_q, sb_k, sb_v, valid)
    y_gla = _gla(gla_q, gla_k, gla_v, gla_r, gla_code, gla_w_gate_up, gla_b_gate, gla_norm)
    y_ml = _mlstm(ml_qk, ml_v, ml_i, ml_f, ml_o, ml_conv, ml_b_i, ml_b_f, ml_norm)
    y_ret = _retention(ret_q, ret_k, ret_v, ret_g, positions, ret_norm)
    gates = jax.nn.sigmoid(merge_logits.reshape(b, l, N_BRANCH, D_MODEL) + b_merge.astype(F32))
    branches = (y_sb, y_gla, y_ml, y_ret)
    merged = gates[:, :, 0] * (branches[0] @ w_branch[0].astype(F32))
    for n in range(1, N_BRANCH):
        merged = merged + gates[:, :, n] * (branches[n] @ w_branch[n].astype(F32))
    mix_out = (merged @ w_out.astype(F32)).astype(h.dtype)
    h = h + _rmsnorm(mix_out, g_mix_post)
    f = _rmsnorm(h, g_ffn_pre)
    ffn_out = (jax.nn.silu(f @ ffn_w_gate) * (f @ ffn_w_up)) @ ffn_w_down
    return h + _rmsnorm(ffn_out.astype(h.dtype), g_ffn_post)


def setup_inputs(seed: int = 0) -> dict:
    key = jax.random.key(seed)
    ks = jax.random.split(key, 24)

    def nrm(k, shape, scale):
        return jax.random.normal(k, shape, F32) * scale

    def gain(k, shape):
        return 1.0 + 0.02 * jax.random.normal(k, shape, F32)

    return {
        'x': nrm(ks[0], (BATCH, SEQ, D_MODEL), 1.0),
        'meta_tokens': nrm(ks[1], (N_META, D_MODEL), 1.0),
        'norm_mix_pre': gain(ks[2], (DEPTH, D_MODEL)),
        'norm_mix_post': gain(ks[3], (DEPTH, D_MODEL)),
        'norm_ffn_pre': gain(ks[4], (DEPTH, D_MODEL)),
        'norm_ffn_post': gain(ks[5], (DEPTH, D_MODEL)),
        'w_in': nrm(ks[6], (DEPTH, D_MODEL, IN_WIDTH), D_MODEL ** -0.5),
        'gla_w_gate_up': nrm(ks[7], (DEPTH, GLA_LOW_RANK, GLA_HEADS * GLA_DK), GLA_LOW_RANK ** -0.5),
        'gla_b_gate': nrm(ks[8], (DEPTH, GLA_HEADS * GLA_DK), 0.1),
        'gla_norm': gain(ks[9], (DEPTH, GLA_HEADS * GLA_DV)),
        'ml_conv': nrm(ks[10], (DEPTH, CONV_WIDTH, 2 * BRANCH_WIDTH), CONV_WIDTH ** -0.5),
        'ml_b_i': nrm(ks[11], (DEPTH, ML_HEADS), 0.1),
        'ml_b_f': jnp.linspace(3.0, 6.0, ML_HEADS, dtype=F32)[None, :] + nrm(ks[12], (DEPTH, ML_HEADS), 0.1),
        'ml_norm': gain(ks[13], (DEPTH, BRANCH_WIDTH)),
        'ret_norm': gain(ks[14], (DEPTH, BRANCH_WIDTH)),
        'w_branch': nrm(ks[15], (DEPTH, N_BRANCH, BRANCH_WIDTH, D_MODEL), BRANCH_WIDTH ** -0.5),
        'b_merge': nrm(ks[16], (DEPTH, N_BRANCH, D_MODEL), 0.1),
        'w_out': nrm(ks[17], (DEPTH, D_MODEL, D_MODEL), D_MODEL ** -0.5),
        'ffn_w_gate': nrm(ks[18], (DEPTH, D_MODEL, D_FF), D_MODEL ** -0.5),
        'ffn_w_up': nrm(ks[19], (DEPTH, D_MODEL, D_FF), D_MODEL ** -0.5),
        'ffn_w_down': nrm(ks[20], (DEPTH, D_FF, D_MODEL), D_FF ** -0.5),
    }


def reference(x, meta_tokens, norm_mix_pre, norm_mix_post, norm_ffn_pre, norm_ffn_post, w_in,
              gla_w_gate_up, gla_b_gate, gla_norm, ml_conv, ml_b_i, ml_b_f, ml_norm, ret_norm,
              w_branch, b_merge, w_out, ffn_w_gate, ffn_w_up, ffn_w_down):
    b = x.shape[0]
    pad = jnp.zeros((b, N_FRONT_PAD, D_MODEL), x.dtype)
    meta = jnp.broadcast_to(meta_tokens.astype(x.dtype)[None], (b, N_META, D_MODEL))
    h = jnp.concatenate([pad, meta, x], axis=1)
    positions = jnp.arange(h.shape[1], dtype=jnp.int32) - N_FRONT_PAD
    valid = positions >= 0
    for layer in range(DEPTH):
        h = _hybrid_layer(h, valid, positions,
                          norm_mix_pre[layer], norm_mix_post[layer], norm_ffn_pre[layer], norm_ffn_post[layer],
                          w_in[layer], gla_w_gate_up[layer], gla_b_gate[layer], gla_norm[layer],
                          ml_conv[layer], ml_b_i[layer], ml_b_f[layer], ml_norm[layer], ret_norm[layer],
                          w_branch[layer], b_merge[layer], w_out[layer],
                          ffn_w_gate[layer], ffn_w_up[layer], ffn_w_down[layer])
    return h[:, BLOCK:, :]
```

```python
import functools

import jax
import jax.numpy as jnp
from jax import lax
from jax.experimental import pallas as pl
from jax.experimental.pallas import tpu as pltpu

F32 = jnp.float32
BF16 = jnp.bfloat16

D_MODEL = 1024
BLOCK = 128
HEAD_DIM = 64
N_HEADS = 4
BW = N_HEADS * HEAD_DIM
N_BRANCH = 4
N_META = 16
N_FRONT_PAD = BLOCK - N_META
GLA_DK = 32
GLA_LOW_RANK = 16
GLA_GATE_NORMALIZER = 16.0
CONV_WIDTH = 4
D_FF = 2816
FF_CHUNK = 256
ROPE_BASE = 10000.0
NORM_EPS = 1e-6
SB_LOG_ZERO = -104.0
V7X_VMEM_LIMIT = 56 * 1024 * 1024

_C = {}
_off = 0
for _name, _w in (("sb", 768), ("gla_q", 128), ("gla_k", 128), ("gla_v", 256), ("gla_r", 256), ("gla_code", 16),
                  ("ml_qk", 512), ("ml_v", 256), ("ml_i", 4), ("ml_f", 4), ("ml_o", 256), ("ret", 1024),
                  ("merge", 4096)):
    _C[_name] = (_off, _off + _w)
    _off += _w
IN_WIDTH = _off


def _mm(a, b):
    return jnp.dot(a, b, preferred_element_type=F32)


def _mm_nt(a, b):
    return lax.dot_general(a, b, (((1,), (1,)), ((), ())), preferred_element_type=F32)


def _split(x):
    hi = x.astype(BF16)
    lo = (x - hi.astype(F32)).astype(BF16)
    return hi, lo


def _mm_split_l(x, c):
    hi, lo = _split(x)
    return _mm(hi, c) + _mm(lo, c)


def _mm_split_r(c, x):
    hi, lo = _split(x)
    return _mm(c, hi) + _mm(c, lo)


def _log_sigmoid(x):
    return -(jnp.maximum(-x, 0.0) + jnp.log1p(jnp.exp(-jnp.abs(x))))


def _sigmoid(x):
    return 1.0 / (1.0 + jnp.exp(-x))


def _rmsnorm(x, g):
    return x * lax.rsqrt(jnp.mean(x * x, axis=-1, keepdims=True) + NORM_EPS) * g


def _head_rmsnorm(y, g, bdmean):
    ms = _mm_split_l(y * y, bdmean)
    return y * lax.rsqrt(ms + NORM_EPS) * g


def _head_sel(width, per_head, h):
    lane = lax.broadcasted_iota(jnp.int32, (1, width), 1)
    return (lane >= per_head * h) & (lane < per_head * (h + 1))


def _row_stack_heads(v_b):
    return jnp.concatenate([jnp.where(_head_sel(BW, HEAD_DIM, h), v_b, jnp.zeros_like(v_b))
                            for h in range(N_HEADS)], axis=0)


def _proj_kernel(h_ref, g_ref, wsb_ref, wgla_ref, wml_ref, wret_ref,
                 hn_ref, usb_ref, ugla_ref, uml_ref, uret_ref, *, tm):
    x = h_ref[...]
    pos = pl.program_id(1) * tm + lax.broadcasted_iota(jnp.int32, (tm, 1), 0)
    valid = (pos >= N_FRONT_PAD).astype(F32)
    hn = (_rmsnorm(x, g_ref[...]) * valid).astype(BF16)
    hn_ref[...] = hn
    usb_ref[...] = _mm(hn, wsb_ref[...]).astype(BF16)
    ugla_ref[...] = _mm(hn, wgla_ref[...])
    uml_ref[...] = _mm(hn, wml_ref[...])
    uret_ref[...] = _mm(hn, wret_ref[...])


def _sb_kernel(q_ref, k_ref, v_ref, uo_ref, o_ref, acc_ref, cb_ref):
    i = pl.program_id(1)
    q = q_ref[...]
    qm = [jnp.where(_head_sel(BW, HEAD_DIM, h), q, jnp.zeros_like(q)) for h in range(N_HEADS)]
    row = lax.broadcasted_iota(jnp.int32, (BLOCK, BLOCK), 0)
    col = lax.broadcasted_iota(jnp.int32, (BLOCK, BLOCK), 1)
    uo = uo_ref[...]
    acc_ref[...] = jnp.zeros_like(acc_ref)
    cb_ref[...] = jnp.zeros_like(cb_ref)
    scale = HEAD_DIM ** -0.5

    def cond(c):
        j, mx = c
        return jnp.logical_and(j >= 0, mx > SB_LOG_ZERO)

    def body(c):
        j, _ = c
        start = pl.multiple_of(j * BLOCK, BLOCK)
        kj = k_ref[pl.ds(start, BLOCK), :]
        vj = v_ref[pl.ds(start, BLOCK), :]
        kpos = j * BLOCK + col
        mask = (kpos < i * BLOCK + row) & (kpos >= N_FRONT_PAD)
        a_parts = []
        mx = None
        for h in range(N_HEADS):
            z = _mm_nt(qm[h], kj) * scale
            lk = jnp.where(mask, -(jnp.maximum(z, 0.0) + jnp.log1p(jnp.exp(-jnp.abs(z)))), 0.0)
            r = _mm_split_l(lk, uo)
            cb = cb_ref[h]
            a = jnp.where(mask, jnp.exp(z + r[:, :BLOCK] + cb), 0.0)
            a_parts.append(a.astype(BF16))
            cb = cb + r[:, BLOCK:]
            cb_ref[h] = cb
            mx = cb if mx is None else jnp.maximum(mx, cb)
        acc_ref[...] += _mm(jnp.concatenate(a_parts, axis=1), _row_stack_heads(vj))
        m = jnp.max(jnp.max(mx, axis=1, keepdims=True), axis=0, keepdims=True)
        return j - 1, m[0, 0]

    lax.while_loop(cond, body, (i, jnp.float32(0.0)))
    o_ref[...] = acc_ref[...]


def _gla_kernel(u_ref, wg_ref, bg_ref, g_ref, tril_ref, bd_ref, bdmean_ref, o_ref, st_ref):
    @pl.when(pl.program_id(1) == 0)
    def _():
        st_ref[...] = jnp.zeros_like(st_ref)

    q = u_ref[:, 0:128] * (GLA_DK ** -0.5)
    k = u_ref[:, 128:256]
    v = u_ref[:, 256:512]
    r = u_ref[:, 512:768]
    code = u_ref[:, 768:896]
    log_a = _log_sigmoid(_mm(code.astype(BF16), wg_ref[...]) + bg_ref[...]) * (1.0 / GLA_GATE_NORMALIZER)
    cum = _mm_split_r(tril_ref[...], log_a)
    cum_last = cum[BLOCK - 1:BLOCK, :]
    qd = (q * jnp.exp(cum)).astype(BF16)
    kd = (k * jnp.exp(-cum)).astype(BF16)
    ke = (k * jnp.exp(cum_last - cum)).astype(BF16)
    st = st_ref[...]
    inter = _mm_nt(qd, st.astype(BF16))
    row = lax.broadcasted_iota(jnp.int32, (BLOCK, BLOCK), 0)
    col = lax.broadcasted_iota(jnp.int32, (BLOCK, BLOCK), 1)
    parts = []
    for h in range(N_HEADS):
        qh = jnp.where(_head_sel(BLOCK, GLA_DK, h), qd, jnp.zeros_like(qd))
        parts.append(jnp.where(row >= col, _mm_nt(qh, kd), 0.0).astype(BF16))
    v_b = v.astype(BF16)
    o = _mm(jnp.concatenate(parts, axis=1), _row_stack_heads(v_b)) + inter
    local = _mm(v.T.astype(BF16), ke)
    st_ref[...] = st * jnp.exp(cum_last) + local * bd_ref[...]
    o_ref[...] = _head_rmsnorm(o, g_ref[...], bdmean_ref[...]) * (r * _sigmoid(r))


def _rotary(x, cos, sin_signed):
    first_half = _head_sel(BLOCK, 32, 0) | _head_sel(BLOCK, 32, 2)
    outs = []
    for t in range(2):
        xt = x[:, BLOCK * t:BLOCK * (t + 1)]
        swapped = jnp.where(first_half, pltpu.roll(xt, 96, 1), pltpu.roll(xt, 32, 1))
        outs.append(xt * cos[:, BLOCK * t:BLOCK * (t + 1)] + swapped * sin_signed[:, BLOCK * t:BLOCK * (t + 1)])
    return jnp.concatenate(outs, axis=1)


def _ret_kernel(u_ref, cos_ref, sin_ref, dec_ref, qdec_ref, kdec_ref, gam_ref, g_ref, bd_ref, bdmean_ref,
                o_ref, st_ref):
    @pl.when(pl.program_id(1) == 0)
    def _():
        st_ref[...] = jnp.zeros_like(st_ref)

    cos = cos_ref[...]
    sin = sin_ref[...]
    qr = _rotary(u_ref[:, 0:256], cos, sin)
    kr = _rotary(u_ref[:, 256:512], cos, sin) * (HEAD_DIM ** -0.5)
    v = u_ref[:, 512:768]
    g = u_ref[:, 768:1024]
    qr_b = qr.astype(BF16)
    kr_b = kr.astype(BF16)
    parts = []
    for h in range(N_HEADS):
        qh = jnp.where(_head_sel(BW, HEAD_DIM, h), qr_b, jnp.zeros_like(qr_b))
        parts.append((_mm_nt(qh, kr_b) * dec_ref[h]).astype(BF16))
    v_b = v.astype(BF16)
    st = st_ref[...]
    o = (_mm(jnp.concatenate(parts, axis=1), _row_stack_heads(v_b))
         + _mm_nt((qr * qdec_ref[...]).astype(BF16), st.astype(BF16)))
    local = _mm(v.T.astype(BF16), (kr * kdec_ref[...]).astype(BF16))
    st_ref[...] = st * gam_ref[...] + local * bd_ref[...]
    o_ref[...] = (g * _sigmoid(g)) * _head_rmsnorm(o, g_ref[...], bdmean_ref[...])


def _spread_head(x_t, h):
    lane = lax.broadcasted_iota(jnp.int32, (1, BLOCK), 1)
    own = (lane < HEAD_DIM) if h % 2 == 0 else (lane >= HEAD_DIM)
    return jnp.where(own, x_t, pltpu.roll(x_t, HEAD_DIM, 1))


def _ml_kernel(u_ref, cw_ref, bi_ref, bf_ref, g_ref, tril_ref, bd_ref, bdones_ref, onestack_ref, bdmean_ref,
               o_ref, xbuf_ref, ct_ref, n_ref, m_ref):
    @pl.when(pl.program_id(1) == 0)
    def _():
        xbuf_ref[0:8, :] = jnp.zeros((8, 2 * BW), F32)
        ct_ref[...] = jnp.zeros_like(ct_ref)
        n_ref[...] = jnp.zeros_like(n_ref)
        m_ref[...] = jnp.zeros_like(m_ref)

    xbuf_ref[8:8 + BLOCK, :] = u_ref[:, 0:512]
    conv = None
    for j in range(CONV_WIDTH):
        term = cw_ref[j:j + 1, :] * xbuf_ref[8 - (CONV_WIDTH - 1) + j:8 - (CONV_WIDTH - 1) + j + BLOCK, :]
        conv = term if conv is None else conv + term
    xbuf_ref[0:8, :] = xbuf_ref[BLOCK:BLOCK + 8, :]
    qk = conv * _sigmoid(conv)
    q = qk[:, 0:BW]
    k = qk[:, BW:2 * BW] * (HEAD_DIM ** -0.5)
    v = u_ref[:, 512:768]
    log_i = u_ref[:, 768:1024] + bi_ref[...]
    log_f = _log_sigmoid(u_ref[:, 1024:1280] + bf_ref[...])
    o_pre = u_ref[:, 1280:1536]

    cum = _mm_split_r(tril_ref[...], log_f)
    cum_last = cum[BLOCK - 1:BLOCK, :]
    w_end = cum_last - cum + log_i
    a_end = jnp.max(w_end, axis=0, keepdims=True)
    kp = k * jnp.exp(w_end - a_end)
    v_b = v.astype(BF16)
    c_loc = _mm(v.T.astype(BF16), kp.astype(BF16)) * bd_ref[...]
    n_loc = jnp.sum(kp, axis=0, keepdims=True)

    m_s = m_ref[0:1, :]
    n_s = n_ref[0:1, :]
    ct = ct_ref[...]
    inter_log = cum + m_s
    d_t = (log_i - cum).T
    row = lax.broadcasted_iota(jnp.int32, (BLOCK, BLOCK), 0)
    col = lax.broadcasted_iota(jnp.int32, (BLOCK, BLOCK), 1)
    lane = lax.broadcasted_iota(jnp.int32, (1, BLOCK), 1)
    q_b = q.astype(BF16)
    k_b = k.astype(BF16)
    s_parts, m_parts = [], []
    for h in range(N_HEADS):
        t = h // 2
        cq = _spread_head(cum[:, BLOCK * t:BLOCK * (t + 1)], h)
        ilq = _spread_head(inter_log[:, BLOCK * t:BLOCK * (t + 1)], h)
        dk = jnp.broadcast_to(d_t[HEAD_DIM * h:HEAD_DIM * h + 1, :], (BLOCK, BLOCK))
        intra_log = jnp.where(row >= col, cq + dk, -jnp.inf)
        m_h = jnp.maximum(ilq, jnp.max(intra_log, axis=1, keepdims=True))
        qh = jnp.where(_head_sel(BW, HEAD_DIM, h), q_b, jnp.zeros_like(q_b))
        s_parts.append(_mm_nt(qh, k_b) * jnp.exp(intra_log - m_h))
        m_parts.append(m_h)
    m_t = jnp.concatenate([jnp.where(lane < HEAD_DIM, m_parts[0], m_parts[1]),
                           jnp.where(lane < HEAD_DIM, m_parts[2], m_parts[3])], axis=1)
    w_inter = jnp.exp(inter_log - m_t)
    s_hi, s_lo = _split(jnp.concatenate(s_parts, axis=1))
    onestack = onestack_ref[...]
    nd = _mm(s_hi, jnp.concatenate([_row_stack_heads(v_b), onestack], axis=1))
    num = nd[:, 0:BW] + w_inter * _mm_nt(q_b, ct.astype(BF16))
    den = nd[:, BW:2 * BW] + _mm(s_lo, onestack) + w_inter * _mm_split_l(q * n_s, bdones_ref[...])
    h_cell = num / jnp.maximum(jnp.abs(den), jnp.exp(-m_t))
    o_ref[...] = _sigmoid(o_pre) * _head_rmsnorm(h_cell, g_ref[...], bdmean_ref[...])

    m_new = jnp.maximum(cum_last + m_s, a_end)
    s_prev = jnp.exp(cum_last + m_s - m_new)
    s_loc = jnp.exp(a_end - m_new)
    ct_ref[...] = s_prev * ct + s_loc * c_loc
    n_ref[...] = jnp.broadcast_to(s_prev * n_s + s_loc * n_loc, n_ref.shape)
    m_ref[...] = jnp.broadcast_to(m_new, m_ref.shape)


def _merge_kernel(hn_ref, h_ref, ysb_ref, ygla_ref, yml_ref, yret_ref, wm_ref, bm_ref, wb_ref, wo_ref, g_ref,
                  o_ref):
    hn = hn_ref[...]
    merged = None
    for n, y_ref in enumerate((ysb_ref, ygla_ref, yml_ref, yret_ref)):
        cols = slice(D_MODEL * n, D_MODEL * (n + 1))
        gate = _sigmoid(_mm(hn, wm_ref[:, cols]) + bm_ref[:, cols])
        term = gate * _mm(y_ref[...].astype(BF16), wb_ref[n])
        merged = term if merged is None else merged + term
    mix_out = _mm(merged.astype(BF16), wo_ref[...])
    o_ref[...] = h_ref[...] + _rmsnorm(mix_out, g_ref[...])


def _ffn_kernel(h_ref, gpre_ref, wg_ref, wu_ref, wd_ref, gpost_ref, o_ref, acc_ref):
    x = h_ref[...]
    f = _rmsnorm(x, gpre_ref[...]).astype(BF16)
    for c in range(D_FF // FF_CHUNK):
        cols = slice(FF_CHUNK * c, FF_CHUNK * (c + 1))
        a = _mm(f, wg_ref[:, cols])
        act = ((a * _sigmoid(a)) * _mm(f, wu_ref[:, cols])).astype(BF16)
        part = _mm(act, wd_ref[cols, :])
        if c == 0:
            acc_ref[...] = part
        else:
            acc_ref[...] += part
    o_ref[...] = x + _rmsnorm(acc_ref[...], gpost_ref[...])


def _row_tile(lt):
    for tm in (640, 512, 384, 256, 128):
        if lt % tm == 0:
            return tm
    raise ValueError(f"sequence length {lt} is not a multiple of {BLOCK}")


def _const_spec(shape):
    nd = len(shape)
    return pl.BlockSpec(shape, lambda b, j: (0,) * nd)


def _params(sem):
    return pltpu.CompilerParams(dimension_semantics=sem, vmem_limit_bytes=V7X_VMEM_LIMIT)


def _row_call(kernel, name, batch, lt, row_inputs, const_inputs, out_widths, out_dtypes, scratch=()):
    tm = _row_tile(lt)
    nb = lt // tm
    rows = batch * lt
    row_spec = lambda w: pl.BlockSpec((tm, w), lambda b, j: (b * nb + j, 0))
    return pl.pallas_call(
        kernel,
        grid=(batch, nb),
        in_specs=[row_spec(a.shape[1]) for a in row_inputs] + [_const_spec(a.shape) for a in const_inputs],
        out_specs=[row_spec(w) for w in out_widths],
        out_shape=[jax.ShapeDtypeStruct((rows, w), dt) for w, dt in zip(out_widths, out_dtypes)],
        scratch_shapes=list(scratch),
        compiler_params=_params(("parallel", "parallel")),
        name=name,
    )(*row_inputs, *const_inputs)


def _chunk_call(kernel, name, batch, lt, u, chunk_inputs, const_inputs, scratch):
    nc = lt // BLOCK
    return pl.pallas_call(
        kernel,
        grid=(batch, nc),
        in_specs=([pl.BlockSpec((BLOCK, u.shape[1]), lambda b, n: (b * nc + n, 0))]
                  + [pl.BlockSpec((BLOCK, a.shape[1]), lambda b, n: (n, 0)) for a in chunk_inputs]
                  + [_const_spec(a.shape) for a in const_inputs]),
        out_specs=pl.BlockSpec((BLOCK, BW), lambda b, n: (b * nc + n, 0)),
        out_shape=jax.ShapeDtypeStruct((batch * lt, BW), F32),
        scratch_shapes=list(scratch),
        compiler_params=_params(("arbitrary", "arbitrary")),
        name=name,
    )(u, *chunk_inputs, *const_inputs)


def _sb_call(u_sb, uo, batch, lt):
    nc = lt // BLOCK
    return pl.pallas_call(
        _sb_kernel,
        grid=(batch, nc),
        in_specs=[pl.BlockSpec((BLOCK, BW), lambda b, i: (b * nc + i, 0)),
                  pl.BlockSpec((lt, BW), lambda b, i: (b, 1)),
                  pl.BlockSpec((lt, BW), lambda b, i: (b, 2)),
                  _const_spec(uo.shape)],
        out_specs=pl.BlockSpec((BLOCK, BW), lambda b, i: (b * nc + i, 0)),
        out_shape=jax.ShapeDtypeStruct((batch * lt, BW), F32),
        scratch_shapes=[pltpu.VMEM((BLOCK, BW), F32), pltpu.VMEM((N_HEADS, BLOCK, BLOCK), F32)],
        compiler_params=_params(("parallel", "parallel")),
        name="stick_breaking",
    )(u_sb, u_sb, u_sb, uo)


def _constants(lt):
    idx = jnp.arange(BLOCK)
    tril = (idx[:, None] >= idx[None, :])
    c = {"tril": tril.astype(BF16),
         "uo": jnp.concatenate([tril, jnp.ones((BLOCK, BLOCK), bool)], axis=1).astype(BF16)}
    head_e = jnp.arange(BW) // HEAD_DIM
    c["bd_gla"] = (head_e[:, None] == (jnp.arange(BLOCK) // GLA_DK)[None, :]).astype(F32)
    bd = head_e[:, None] == head_e[None, :]
    c["bd"] = bd.astype(F32)
    c["bdones"] = bd.astype(BF16)
    c["bdmean"] = (bd.astype(F32) / HEAD_DIM).astype(BF16)
    c["onestack"] = ((jnp.arange(N_HEADS * BLOCK) // BLOCK)[:, None] == head_e[None, :]).astype(BF16)
    positions = jnp.arange(lt, dtype=jnp.int32) - N_FRONT_PAD
    half = HEAD_DIM // 2
    inv_freq = ROPE_BASE ** (-jnp.arange(half, dtype=F32) / half)
    ang = positions.astype(F32)[:, None] * inv_freq[None, :]
    cos, sin = jnp.cos(ang), jnp.sin(ang)
    c["cos"] = jnp.tile(jnp.concatenate([cos, cos], axis=1), (1, N_HEADS))
    c["sin"] = jnp.tile(jnp.concatenate([-sin, sin], axis=1), (1, N_HEADS))
    log_gamma = jnp.log1p(-jnp.exp2(-5.0 - jnp.arange(N_HEADS, dtype=F32)))
    pos = jnp.arange(BLOCK, dtype=F32)
    diff = jnp.maximum(pos[:, None] - pos[None, :], 0.0)
    c["ret_dec"] = jnp.where(tril, jnp.exp(log_gamma[:, None, None] * diff), 0.0)
    c["ret_qdec"] = jnp.repeat(jnp.exp(log_gamma[:, None] * (pos + 1.0)).T, HEAD_DIM, axis=1)
    c["ret_kdec"] = jnp.repeat(jnp.exp(log_gamma[:, None] * (BLOCK - 1.0 - pos)).T, HEAD_DIM, axis=1)
    c["ret_gam"] = jnp.repeat(jnp.exp(log_gamma * BLOCK), HEAD_DIM)[None, :]
    return c


def _cols(w, name):
    a, b = _C[name]
    return w[..., a:b]


def _layout_weights(w_in):
    pad = jnp.zeros(w_in.shape[:-1] + (BLOCK - GLA_LOW_RANK,), w_in.dtype)
    rep = lambda name: jnp.repeat(_cols(w_in, name), HEAD_DIM, axis=-1)
    groups = {
        "sb": _cols(w_in, "sb"),
        "gla": jnp.concatenate([_cols(w_in, n) for n in ("gla_q", "gla_k", "gla_v", "gla_r", "gla_code")] + [pad],
                               axis=-1),
        "ml": jnp.concatenate([_cols(w_in, "ml_qk"), _cols(w_in, "ml_v"), rep("ml_i"), rep("ml_f"),
                               _cols(w_in, "ml_o")], axis=-1),
        "ret": _cols(w_in, "ret"),
        "merge": _cols(w_in, "merge"),
    }
    return {k: v.astype(BF16) for k, v in groups.items()}


def kernel(x, meta_tokens, norm_mix_pre, norm_mix_post, norm_ffn_pre, norm_ffn_post, w_in,
           gla_w_gate_up, gla_b_gate, gla_norm, ml_conv, ml_b_i, ml_b_f, ml_norm, ret_norm,
           w_branch, b_merge, w_out, ffn_w_gate, ffn_w_up, ffn_w_down):
    batch, seq, d = x.shape
    depth = w_in.shape[0]
    lt = seq + BLOCK
    pad = jnp.zeros((batch, N_FRONT_PAD, d), x.dtype)
    meta = jnp.broadcast_to(meta_tokens.astype(x.dtype)[None], (batch, N_META, d))
    h = jnp.concatenate([pad, meta, x], axis=1).reshape(batch * lt, d)

    c = _constants(lt)
    w = _layout_weights(w_in)
    wg_gla = jnp.pad(gla_w_gate_up, ((0, 0), (0, BLOCK - GLA_LOW_RANK), (0, 0))).astype(BF16)
    wb = w_branch.astype(BF16)
    wo = w_out.astype(BF16)
    wfg, wfu, wfd = ffn_w_gate.astype(BF16), ffn_w_up.astype(BF16), ffn_w_down.astype(BF16)
    bi = jnp.repeat(ml_b_i, HEAD_DIM, axis=-1)
    bf = jnp.repeat(ml_b_f, HEAD_DIM, axis=-1)
    tm = _row_tile(lt)

    for l in range(depth):
        hn, u_sb, u_gla, u_ml, u_ret = _row_call(
            functools.partial(_proj_kernel, tm=tm), "in_projection", batch, lt,
            [h], [norm_mix_pre[l][None], w["sb"][l], w["gla"][l], w["ml"][l], w["ret"][l]],
            [D_MODEL, 768, 896, 1536, 1024], [BF16, BF16, F32, F32, F32])
        y_sb = _sb_call(u_sb, c["uo"], batch, lt)
        y_gla = _chunk_call(
            _gla_kernel, "gla", batch, lt, u_gla, [],
            [wg_gla[l], gla_b_gate[l][None], gla_norm[l][None], c["tril"], c["bd_gla"], c["bdmean"]],
            [pltpu.VMEM((BW, BLOCK), F32)])
        y_ml = _chunk_call(
            _ml_kernel, "mlstm", batch, lt, u_ml, [],
            [ml_conv[l], bi[l][None], bf[l][None], ml_norm[l][None], c["tril"], c["bd"], c["bdones"],
             c["onestack"], c["bdmean"]],
            [pltpu.VMEM((BLOCK + 8, 2 * BW), F32), pltpu.VMEM((BW, BW), F32), pltpu.VMEM((8, BW), F32),
             pltpu.VMEM((8, BW), F32)])
        y_ret = _chunk_call(
            _ret_kernel, "retention", batch, lt, u_ret, [c["cos"], c["sin"]],
            [c["ret_dec"], c["ret_qdec"], c["ret_kdec"], c["ret_gam"], ret_norm[l][None], c["bd"], c["bdmean"]],
            [pltpu.VMEM((BW, BW), F32)])
        (h,) = _row_call(
            _merge_kernel, "merge", batch, lt, [hn, h, y_sb, y_gla, y_ml, y_ret],
            [w["merge"][l], b_merge[l].reshape(1, N_BRANCH * D_MODEL), wb[l], wo[l], norm_mix_post[l][None]],
            [D_MODEL], [F32])
        (h,) = _row_call(
            _ffn_kernel, "swiglu", batch, lt, [h],
            [norm_ffn_pre[l][None], wfg[l], wfu[l], wfd[l], norm_ffn_post[l][None]],
            [D_MODEL], [F32], scratch=[pltpu.VMEM((tm, D_MODEL), F32)])

    return h.reshape(batch, lt, d)[:, BLOCK:, :]
```

```python
import functools

import jax
import jax.numpy as jnp
from jax import lax
from jax.experimental import pallas as pl
from jax.experimental.pallas import tpu as pltpu

F32 = jnp.float32
BF16 = jnp.bfloat16

D_MODEL = 1024
BLOCK = 128
HEAD_DIM = 64
N_HEADS = 4
BW = N_HEADS * HEAD_DIM
N_BRANCH = 4
N_META = 16
N_FRONT_PAD = BLOCK - N_META
GLA_DK = 32
GLA_LOW_RANK = 16
GLA_GATE_NORMALIZER = 16.0
CONV_WIDTH = 4
D_FF = 2816
FF_CHUNK = 256
ROPE_BASE = 10000.0
NORM_EPS = 1e-6
SB_LOG_ZERO = -104.0
SB_STATIC_TILES = 3
V7X_VMEM_LIMIT = 56 * 1024 * 1024

_C = {}
_off = 0
for _name, _w in (("sb", 768), ("gla_q", 128), ("gla_k", 128), ("gla_v", 256), ("gla_r", 256), ("gla_code", 16),
                  ("ml_qk", 512), ("ml_v", 256), ("ml_i", 4), ("ml_f", 4), ("ml_o", 256), ("ret", 1024),
                  ("merge", 4096)):
    _C[_name] = (_off, _off + _w)
    _off += _w
IN_WIDTH = _off


def _mm(a, b):
    return jnp.dot(a, b, preferred_element_type=F32)


def _mm_nt(a, b):
    return lax.dot_general(a, b, (((1,), (1,)), ((), ())), preferred_element_type=F32)


def _split(x):
    hi = x.astype(BF16)
    lo = (x - hi.astype(F32)).astype(BF16)
    return hi, lo


def _mm_split_r(c2, x):
    hi, lo = _split(x)
    return _mm(c2, jnp.concatenate([hi, lo], axis=0))


def _softplus(x):
    return jnp.maximum(x, 0.0) + jnp.log(1.0 + jnp.exp(-jnp.abs(x)))


def _log_sigmoid(x):
    return -_softplus(-x)


def _sigmoid(x):
    return 1.0 / (1.0 + jnp.exp(-x))


def _rmsnorm(x, g):
    return x * lax.rsqrt(jnp.mean(x * x, axis=-1, keepdims=True) + NORM_EPS) * g


def _head_rmsnorm(y, g, bdmean):
    ms = _mm((y * y).astype(BF16), bdmean)
    return y * lax.rsqrt(ms + NORM_EPS) * g


def _head_sel(width, per_head, h):
    lane = lax.broadcasted_iota(jnp.int32, (1, width), 1)
    return (lane >= per_head * h) & (lane < per_head * (h + 1))


def _row_stack_heads(x, per_head):
    return jnp.concatenate([jnp.where(_head_sel(x.shape[1], per_head, h), x, jnp.zeros_like(x))
                            for h in range(N_HEADS)], axis=0)


def _lane_cat_heads(s):
    return jnp.concatenate([s[BLOCK * h:BLOCK * (h + 1)] for h in range(N_HEADS)], axis=1)


def _tile4(x):
    return jnp.concatenate([x] * N_HEADS, axis=0)


def _proj_kernel(h_ref, g_ref, wsb_ref, wgla_ref, wml_ref, wret_ref,
                 hn_ref, usb_ref, ugla_ref, uml_ref, uret_ref, *, tm):
    x = h_ref[...]
    pos = pl.program_id(1) * tm + lax.broadcasted_iota(jnp.int32, (tm, 1), 0)
    valid = (pos >= N_FRONT_PAD).astype(F32)
    hn = (_rmsnorm(x, g_ref[...]) * valid).astype(BF16)
    hn_ref[...] = hn
    usb_ref[...] = _mm(hn, wsb_ref[...]).astype(BF16)
    ugla_ref[...] = _mm(hn, wgla_ref[...])
    uml_ref[...] = _mm(hn, wml_ref[...])
    uret_ref[...] = _mm(hn, wret_ref[...])


def _sb_tile(q, k_ref, v_ref, j, bias, nuo2, cb):
    return _sb_tile_finish(_sb_tile_sums(q, k_ref, j, bias, nuo2), v_ref, j, cb)


def _sb_tile_sums(q, k_ref, j, bias, nuo2):
    start = pl.multiple_of(jnp.maximum(j, 0) * BLOCK, BLOCK)
    kj = k_ref[pl.ds(start, BLOCK), :]
    zs = []
    for p in range(N_HEADS // 2):
        lanes = slice(BLOCK * p, BLOCK * (p + 1))
        kp = kj[:, lanes]
        k_rows = jnp.concatenate([jnp.where(_head_sel(BLOCK, HEAD_DIM, e), kp, jnp.zeros_like(kp))
                                  for e in range(2)], axis=0)
        zs.append(_mm_nt(q[:, lanes], k_rows))
    z = jnp.concatenate(zs, axis=1) + bias
    hi, lo = _split(_softplus(z))
    logw, totals = [], []
    for h in range(N_HEADS):
        lanes = slice(BLOCK * h, BLOCK * (h + 1))
        r = _mm(jnp.concatenate([hi[:, lanes], lo[:, lanes]], axis=1), nuo2)
        logw.append(z[:, lanes] + r[:, :BLOCK])
        totals.append(r[:, BLOCK:])
    return jnp.concatenate(logw, axis=1), jnp.concatenate(totals, axis=1)


def _sb_tile_finish(sums, v_ref, j, cb):
    logw, totals = sums
    start = pl.multiple_of(jnp.maximum(j, 0) * BLOCK, BLOCK)
    vj = v_ref[pl.ds(start, BLOCK), :]
    a = jnp.exp(logw + cb).astype(BF16)
    return _mm(a, _row_stack_heads(vj, HEAD_DIM)), cb + totals


def _sb_kernel(q_ref, k_ref, v_ref, nuo_ref, o_ref, acc_ref, cb_ref):
    i = pl.program_id(1)
    q = q_ref[...] * jnp.asarray(HEAD_DIM ** -0.5, BF16)
    row = lax.broadcasted_iota(jnp.int32, (BLOCK, BLOCK), 0)
    col = lax.broadcasted_iota(jnp.int32, (BLOCK, BLOCK), 1)
    nuo2 = nuo_ref[...]

    def bias_for(j, causal):
        ok = (j * BLOCK + col) >= N_FRONT_PAD
        if causal:
            ok = ok & (col < row)
        return jnp.concatenate([jnp.where(ok, 0.0, -jnp.inf)] * N_HEADS, axis=1)

    def all_max(x):
        return jnp.max(jnp.max(x, axis=1, keepdims=True), axis=0, keepdims=True)[0, 0]

    cb = jnp.zeros((BLOCK, N_HEADS * BLOCK), F32)
    out = None
    for t in range(SB_STATIC_TILES):
        contrib, cb = _sb_tile(q, k_ref, v_ref, i - t, bias_for(i - t, t == 0), nuo2, cb)
        out = contrib if out is None else out + contrib
    acc_ref[...] = out
    cb_ref[...] = cb

    def cond(c):
        j, mx = c
        return jnp.logical_and(j >= 0, mx > SB_LOG_ZERO)

    def body(c):
        j, _ = c
        contrib, cbn = _sb_tile(q, k_ref, v_ref, j, bias_for(j, False), nuo2, cb_ref[...])
        acc_ref[...] += contrib
        cb_ref[...] = cbn
        return j - 1, all_max(cbn)

    lax.while_loop(cond, body, (i - SB_STATIC_TILES, all_max(cb)))
    o_ref[...] = acc_ref[...]


def _gla_chunk(u_ref, wg_ref, bg_ref, g_ref, tril_ref, bd_ref, bdmean_ref, o_ref, st_ref):
    q = u_ref[:, 0:128] * (GLA_DK ** -0.5)
    k = u_ref[:, 128:256]
    v = u_ref[:, 256:512]
    r = u_ref[:, 512:768]
    code = u_ref[:, 768:896]
    log_a = _log_sigmoid(_mm(code.astype(BF16), wg_ref[...]) + bg_ref[...]) * (1.0 / GLA_GATE_NORMALIZER)
    cum = _mm_split_r(tril_ref[...], log_a)
    cum_last = cum[BLOCK - 1:BLOCK, :]
    qd = (q * jnp.exp(cum)).astype(BF16)
    kd = (k * jnp.exp(-cum)).astype(BF16)
    ke = (k * jnp.exp(cum_last - cum)).astype(BF16)
    st = st_ref[...]
    inter = _mm_nt(qd, st.astype(BF16))
    row = lax.broadcasted_iota(jnp.int32, (BLOCK, BLOCK), 0)
    col = lax.broadcasted_iota(jnp.int32, (BLOCK, BLOCK), 1)
    s = jnp.where(_tile4(row >= col), _mm_nt(_row_stack_heads(qd, GLA_DK), kd), 0.0)
    v_b = v.astype(BF16)
    o = _mm(_lane_cat_heads(s.astype(BF16)), _row_stack_heads(v_b, HEAD_DIM)) + inter
    local = _mm(v.T.astype(BF16), ke)
    st_ref[...] = st * jnp.exp(cum_last) + local * bd_ref[...]
    o_ref[...] = _head_rmsnorm(o, g_ref[...], bdmean_ref[...]) * (r * _sigmoid(r))


def _rotary(x, cos, sin_signed):
    first_half = _head_sel(BLOCK, 32, 0) | _head_sel(BLOCK, 32, 2)
    outs = []
    for t in range(2):
        xt = x[:, BLOCK * t:BLOCK * (t + 1)]
        swapped = jnp.where(first_half, pltpu.roll(xt, 96, 1), pltpu.roll(xt, 32, 1))
        outs.append(xt * cos[:, BLOCK * t:BLOCK * (t + 1)] + swapped * sin_signed[:, BLOCK * t:BLOCK * (t + 1)])
    return jnp.concatenate(outs, axis=1)


def _ret_chunk(u_ref, cos_ref, sin_ref, dec_ref, qdec_ref, kdec_ref, gam_ref, g_ref, bd_ref, bdmean_ref,
               o_ref, st_ref):
    cos = cos_ref[...]
    sin = sin_ref[...]
    qr = _rotary(u_ref[:, 0:256], cos, sin)
    kr = _rotary(u_ref[:, 256:512], cos, sin) * (HEAD_DIM ** -0.5)
    v = u_ref[:, 512:768]
    g = u_ref[:, 768:1024]
    kr_b = kr.astype(BF16)
    s = _mm_nt(_row_stack_heads(qr.astype(BF16), HEAD_DIM), kr_b) * dec_ref[...]
    v_b = v.astype(BF16)
    st = st_ref[...]
    o = (_mm(_lane_cat_heads(s.astype(BF16)), _row_stack_heads(v_b, HEAD_DIM))
         + _mm_nt((qr * qdec_ref[...]).astype(BF16), st.astype(BF16)))
    local = _mm(v.T.astype(BF16), (kr * kdec_ref[...]).astype(BF16))
    st_ref[...] = st * gam_ref[...] + local * bd_ref[...]
    o_ref[...] = (g * _sigmoid(g)) * _head_rmsnorm(o, g_ref[...], bdmean_ref[...])


def _spread_head(x_t, h):
    lane = lax.broadcasted_iota(jnp.int32, (1, BLOCK), 1)
    own = (lane < HEAD_DIM) if h % 2 == 0 else (lane >= HEAD_DIM)
    return jnp.where(own, x_t, pltpu.roll(x_t, HEAD_DIM, 1))


def _ml_chunk(u_ref, cw_ref, bi_ref, bf_ref, g_ref, tril_ref, bd_ref, bdones_ref, onestack_ref, bdmean_ref,
              o_ref, xbuf_ref, ct_ref, n_ref, m_ref):
    xbuf_ref[8:8 + BLOCK, :] = u_ref[:, 0:512]
    conv = None
    for j in range(CONV_WIDTH):
        term = cw_ref[j:j + 1, :] * xbuf_ref[8 - (CONV_WIDTH - 1) + j:8 - (CONV_WIDTH - 1) + j + BLOCK, :]
        conv = term if conv is None else conv + term
    xbuf_ref[0:8, :] = xbuf_ref[BLOCK:BLOCK + 8, :]
    qk = conv * _sigmoid(conv)
    q = qk[:, 0:BW]
    k = qk[:, BW:2 * BW] * (HEAD_DIM ** -0.5)
    v = u_ref[:, 512:768]
    log_i = u_ref[:, 768:1024] + bi_ref[...]
    log_f = _log_sigmoid(u_ref[:, 1024:1280] + bf_ref[...])
    o_pre = u_ref[:, 1280:1536]

    cum = _mm_split_r(tril_ref[...], log_f)
    cum_last = cum[BLOCK - 1:BLOCK, :]
    w_end = cum_last - cum + log_i
    a_end = jnp.max(w_end, axis=0, keepdims=True)
    kp = k * jnp.exp(w_end - a_end)
    v_b = v.astype(BF16)
    c_loc = _mm(v.T.astype(BF16), kp.astype(BF16)) * bd_ref[...]
    n_loc = jnp.sum(kp, axis=0, keepdims=True)

    m_s = m_ref[0:1, :]
    n_s = n_ref[0:1, :]
    ct = ct_ref[...]
    inter_log = cum + m_s
    d_t = (log_i - cum).T
    row = lax.broadcasted_iota(jnp.int32, (BLOCK, BLOCK), 0)
    col = lax.broadcasted_iota(jnp.int32, (BLOCK, BLOCK), 1)
    lane = lax.broadcasted_iota(jnp.int32, (1, BLOCK), 1)
    q_b = q.astype(BF16)
    k_b = k.astype(BF16)
    qk_all = _mm_nt(_row_stack_heads(q_b, HEAD_DIM), k_b)
    s_parts, m_parts = [], []
    for h in range(N_HEADS):
        t = h // 2
        cq = _spread_head(cum[:, BLOCK * t:BLOCK * (t + 1)], h)
        ilq = _spread_head(inter_log[:, BLOCK * t:BLOCK * (t + 1)], h)
        dk = jnp.broadcast_to(d_t[HEAD_DIM * h:HEAD_DIM * h + 1, :], (BLOCK, BLOCK))
        intra_log = jnp.where(row >= col, cq + dk, -jnp.inf)
        m_h = jnp.maximum(ilq, jnp.max(intra_log, axis=1, keepdims=True))
        s_parts.append(qk_all[BLOCK * h:BLOCK * (h + 1)] * jnp.exp(intra_log - m_h))
        m_parts.append(m_h)
    m_t = jnp.concatenate([jnp.where(lane < HEAD_DIM, m_parts[0], m_parts[1]),
                           jnp.where(lane < HEAD_DIM, m_parts[2], m_parts[3])], axis=1)
    w_inter = jnp.exp(inter_log - m_t)
    s_b = jnp.concatenate(s_parts, axis=1).astype(BF16)
    nd = _mm(s_b, jnp.concatenate([_row_stack_heads(v_b, HEAD_DIM), onestack_ref[...]], axis=1))
    num = nd[:, 0:BW] + w_inter * _mm_nt(q_b, ct.astype(BF16))
    den = nd[:, BW:2 * BW] + w_inter * _mm((q * n_s).astype(BF16), bdones_ref[...])
    h_cell = num / jnp.maximum(jnp.abs(den), jnp.exp(-m_t))
    o_ref[...] = _sigmoid(o_pre) * _head_rmsnorm(h_cell, g_ref[...], bdmean_ref[...])

    m_new = jnp.maximum(cum_last + m_s, a_end)
    s_prev = jnp.exp(cum_last + m_s - m_new)
    s_loc = jnp.exp(a_end - m_new)
    ct_ref[...] = s_prev * ct + s_loc * c_loc
    n_ref[...] = jnp.broadcast_to(s_prev * n_s + s_loc * n_loc, n_ref.shape)
    m_ref[...] = jnp.broadcast_to(m_new, m_ref.shape)


def _recurrent_kernel(ugla_ref, uml_ref, uret_ref, cos_ref, sin_ref,
                      wg_ref, bg_ref, ggla_ref, cw_ref, bi_ref, bf_ref, gml_ref, gret_ref,
                      tril_ref, bdgla_ref, bd_ref, bdones_ref, onestack_ref, bdmean_ref,
                      dec_ref, qdec_ref, kdec_ref, gam_ref,
                      ygla_ref, yml_ref, yret_ref,
                      stgla_ref, xbuf_ref, ct_ref, n_ref, m_ref, stret_ref):
    @pl.when(pl.program_id(0) == 0)
    def _():
        for ref in (stgla_ref, xbuf_ref, ct_ref, n_ref, m_ref, stret_ref):
            ref[...] = jnp.zeros_like(ref)

    for b in range(ugla_ref.shape[0]):
        _gla_chunk(ugla_ref.at[b], wg_ref, bg_ref, ggla_ref, tril_ref, bdgla_ref, bdmean_ref,
                   ygla_ref.at[b], stgla_ref.at[b])
        _ml_chunk(uml_ref.at[b], cw_ref, bi_ref, bf_ref, gml_ref, tril_ref, bd_ref, bdones_ref, onestack_ref,
                  bdmean_ref, yml_ref.at[b], xbuf_ref.at[b], ct_ref.at[b], n_ref.at[b], m_ref.at[b])
        _ret_chunk(uret_ref.at[b], cos_ref, sin_ref, dec_ref, qdec_ref, kdec_ref, gam_ref, gret_ref, bd_ref,
                   bdmean_ref, yret_ref.at[b], stret_ref.at[b])


def _merge_kernel(hn_ref, h_ref, ysb_ref, ygla_ref, yml_ref, yret_ref, wm_ref, bm_ref, wb_ref, wo_ref, g_ref,
                  o_ref):
    hn = hn_ref[...]
    merged = None
    for n, y_ref in enumerate((ysb_ref, ygla_ref, yml_ref, yret_ref)):
        cols = slice(D_MODEL * n, D_MODEL * (n + 1))
        gate = _sigmoid(_mm(hn, wm_ref[:, cols]) + bm_ref[:, cols])
        term = gate * _mm(y_ref[...].astype(BF16), wb_ref[n])
        merged = term if merged is None else merged + term
    mix_out = _mm(merged.astype(BF16), wo_ref[...])
    o_ref[...] = h_ref[...] + _rmsnorm(mix_out, g_ref[...])


def _ffn_kernel(h_ref, gpre_ref, wg_ref, wu_ref, wd_ref, gpost_ref, o_ref, acc_ref):
    x = h_ref[...]
    f = _rmsnorm(x, gpre_ref[...]).astype(BF16)
    for c in range(D_FF // FF_CHUNK):
        cols = slice(FF_CHUNK * c, FF_CHUNK * (c + 1))
        a = _mm(f, wg_ref[:, cols])
        act = ((a * _sigmoid(a)) * _mm(f, wu_ref[:, cols])).astype(BF16)
        part = _mm(act, wd_ref[cols, :])
        if c == 0:
            acc_ref[...] = part
        else:
            acc_ref[...] += part
    o_ref[...] = x + _rmsnorm(acc_ref[...], gpost_ref[...])


def _row_tile(lt):
    for tm in (640, 512, 384, 256, 128):
        if lt % tm == 0:
            return tm
    raise ValueError(f"sequence length {lt} is not a multiple of {BLOCK}")


def _const_spec(shape, grid_rank):
    zeros = (0,) * len(shape)
    return pl.BlockSpec(shape, (lambda b, j: zeros) if grid_rank == 2 else (lambda n: zeros))


def _params(sem):
    return pltpu.CompilerParams(dimension_semantics=sem, vmem_limit_bytes=V7X_VMEM_LIMIT)


def _row_call(kernel, name, batch, lt, row_inputs, const_inputs, out_widths, out_dtypes, scratch=()):
    tm = _row_tile(lt)
    nb = lt // tm
    rows = batch * lt
    row_spec = lambda w: pl.BlockSpec((tm, w), lambda b, j: (b * nb + j, 0))
    return pl.pallas_call(
        kernel,
        grid=(batch, nb),
        in_specs=[row_spec(a.shape[1]) for a in row_inputs] + [_const_spec(a.shape, 2) for a in const_inputs],
        out_specs=[row_spec(w) for w in out_widths],
        out_shape=[jax.ShapeDtypeStruct((rows, w), dt) for w, dt in zip(out_widths, out_dtypes)],
        scratch_shapes=list(scratch),
        compiler_params=_params(("parallel", "parallel")),
        name=name,
    )(*row_inputs, *const_inputs)


def _recurrent_call(batch, lt, u_list, chunk_inputs, const_inputs):
    nc = lt // BLOCK
    u3 = [u.reshape(batch, lt, u.shape[1]) for u in u_list]
    y_spec = pl.BlockSpec((batch, BLOCK, BW), lambda n: (0, n, 0))
    y_gla, y_ml, y_ret = pl.pallas_call(
        _recurrent_kernel,
        grid=(nc,),
        in_specs=([pl.BlockSpec((batch, BLOCK, u.shape[2]), lambda n: (0, n, 0)) for u in u3]
                  + [pl.BlockSpec((BLOCK, a.shape[1]), lambda n: (n, 0)) for a in chunk_inputs]
                  + [_const_spec(a.shape, 1) for a in const_inputs]),
        out_specs=[y_spec] * 3,
        out_shape=[jax.ShapeDtypeStruct((batch, lt, BW), F32)] * 3,
        scratch_shapes=[pltpu.VMEM((batch, BW, BLOCK), F32),
                        pltpu.VMEM((batch, BLOCK + 8, 2 * BW), F32),
                        pltpu.VMEM((batch, BW, BW), F32),
                        pltpu.VMEM((batch, 8, BW), F32),
                        pltpu.VMEM((batch, 8, BW), F32),
                        pltpu.VMEM((batch, BW, BW), F32)],
        compiler_params=_params(("arbitrary",)),
        name="recurrent_mixers",
    )(*u3, *chunk_inputs, *const_inputs)
    return [y.reshape(batch * lt, BW) for y in (y_gla, y_ml, y_ret)]


def _sb_call(u_sb, nuo, batch, lt):
    nc = lt // BLOCK
    return pl.pallas_call(
        _sb_kernel,
        grid=(batch, nc),
        in_specs=[pl.BlockSpec((BLOCK, BW), lambda b, i: (b * nc + i, 0)),
                  pl.BlockSpec((lt, BW), lambda b, i: (b, 1)),
                  pl.BlockSpec((lt, BW), lambda b, i: (b, 2)),
                  _const_spec(nuo.shape, 2)],
        out_specs=pl.BlockSpec((BLOCK, BW), lambda b, i: (b * nc + i, 0)),
        out_shape=jax.ShapeDtypeStruct((batch * lt, BW), F32),
        scratch_shapes=[pltpu.VMEM((BLOCK, BW), F32), pltpu.VMEM((BLOCK, N_HEADS * BLOCK), F32)],
        compiler_params=_params(("parallel", "parallel")),
        name="stick_breaking",
    )(u_sb, u_sb, u_sb, nuo)


def _constants(lt):
    idx = jnp.arange(BLOCK)
    tril = (idx[:, None] >= idx[None, :])
    nuo = -jnp.concatenate([tril, jnp.ones((BLOCK, BLOCK), bool)], axis=1).astype(BF16)
    c = {"tril": jnp.concatenate([tril, tril], axis=1).astype(BF16),
         "nuo": jnp.concatenate([nuo, nuo], axis=0)}
    head_e = jnp.arange(BW) // HEAD_DIM
    c["bd_gla"] = (head_e[:, None] == (jnp.arange(BLOCK) // GLA_DK)[None, :]).astype(F32)
    bd = head_e[:, None] == head_e[None, :]
    c["bd"] = bd.astype(F32)
    c["bdones"] = bd.astype(BF16)
    c["bdmean"] = (bd.astype(F32) / HEAD_DIM).astype(BF16)
    c["onestack"] = ((jnp.arange(N_HEADS * BLOCK) // BLOCK)[:, None] == head_e[None, :]).astype(BF16)
    positions = jnp.arange(lt, dtype=jnp.int32) - N_FRONT_PAD
    half = HEAD_DIM // 2
    inv_freq = ROPE_BASE ** (-jnp.arange(half, dtype=F32) / half)
    ang = positions.astype(F32)[:, None] * inv_freq[None, :]
    cos, sin = jnp.cos(ang), jnp.sin(ang)
    c["cos"] = jnp.tile(jnp.concatenate([cos, cos], axis=1), (1, N_HEADS))
    c["sin"] = jnp.tile(jnp.concatenate([-sin, sin], axis=1), (1, N_HEADS))
    log_gamma = jnp.log1p(-jnp.exp2(-5.0 - jnp.arange(N_HEADS, dtype=F32)))
    pos = jnp.arange(BLOCK, dtype=F32)
    diff = jnp.maximum(pos[:, None] - pos[None, :], 0.0)
    c["ret_dec"] = jnp.where(tril, jnp.exp(log_gamma[:, None, None] * diff), 0.0).reshape(N_HEADS * BLOCK, BLOCK)
    c["ret_qdec"] = jnp.repeat(jnp.exp(log_gamma[:, None] * (pos + 1.0)).T, HEAD_DIM, axis=1)
    c["ret_kdec"] = jnp.repeat(jnp.exp(log_gamma[:, None] * (BLOCK - 1.0 - pos)).T, HEAD_DIM, axis=1)
    c["ret_gam"] = jnp.repeat(jnp.exp(log_gamma * BLOCK), HEAD_DIM)[None, :]
    return c


def _cols(w, name):
    a, b = _C[name]
    return w[..., a:b]


def _layout_weights(w_in):
    pad = jnp.zeros(w_in.shape[:-1] + (BLOCK - GLA_LOW_RANK,), w_in.dtype)
    rep = lambda name: jnp.repeat(_cols(w_in, name), HEAD_DIM, axis=-1)
    groups = {
        "sb": _cols(w_in, "sb"),
        "gla": jnp.concatenate([_cols(w_in, n) for n in ("gla_q", "gla_k", "gla_v", "gla_r", "gla_code")] + [pad],
                               axis=-1),
        "ml": jnp.concatenate([_cols(w_in, "ml_qk"), _cols(w_in, "ml_v"), rep("ml_i"), rep("ml_f"),
                               _cols(w_in, "ml_o")], axis=-1),
        "ret": _cols(w_in, "ret"),
        "merge": _cols(w_in, "merge"),
    }
    return {k: v.astype(BF16) for k, v in groups.items()}


def kernel(x, meta_tokens, norm_mix_pre, norm_mix_post, norm_ffn_pre, norm_ffn_post, w_in,
           gla_w_gate_up, gla_b_gate, gla_norm, ml_conv, ml_b_i, ml_b_f, ml_norm, ret_norm,
           w_branch, b_merge, w_out, ffn_w_gate, ffn_w_up, ffn_w_down):
    batch, seq, d = x.shape
    depth = w_in.shape[0]
    lt = seq + BLOCK
    pad = jnp.zeros((batch, N_FRONT_PAD, d), x.dtype)
    meta = jnp.broadcast_to(meta_tokens.astype(x.dtype)[None], (batch, N_META, d))
    h = jnp.concatenate([pad, meta, x], axis=1).reshape(batch * lt, d)

    c = _constants(lt)
    w = _layout_weights(w_in)
    wg_gla = jnp.pad(gla_w_gate_up, ((0, 0), (0, BLOCK - GLA_LOW_RANK), (0, 0))).astype(BF16)
    wb = w_branch.astype(BF16)
    wo = w_out.astype(BF16)
    wfg, wfu, wfd = ffn_w_gate.astype(BF16), ffn_w_up.astype(BF16), ffn_w_down.astype(BF16)
    bi = jnp.repeat(ml_b_i, HEAD_DIM, axis=-1)
    bf = jnp.repeat(ml_b_f, HEAD_DIM, axis=-1)
    tm = _row_tile(lt)

    for l in range(depth):
        hn, u_sb, u_gla, u_ml, u_ret = _row_call(
            functools.partial(_proj_kernel, tm=tm), "in_projection", batch, lt,
            [h], [norm_mix_pre[l][None], w["sb"][l], w["gla"][l], w["ml"][l], w["ret"][l]],
            [D_MODEL, 768, 896, 1536, 1024], [BF16, BF16, F32, F32, F32])
        y_sb = _sb_call(u_sb, c["nuo"], batch, lt)
        y_gla, y_ml, y_ret = _recurrent_call(
            batch, lt, [u_gla, u_ml, u_ret], [c["cos"], c["sin"]],
            [wg_gla[l], gla_b_gate[l][None], gla_norm[l][None], ml_conv[l], bi[l][None], bf[l][None],
             ml_norm[l][None], ret_norm[l][None], c["tril"], c["bd_gla"], c["bd"], c["bdones"], c["onestack"],
             c["bdmean"], c["ret_dec"], c["ret_qdec"], c["ret_kdec"], c["ret_gam"]])
        (h,) = _row_call(
            _merge_kernel, "merge", batch, lt, [hn, h, y_sb, y_gla, y_ml, y_ret],
            [w["merge"][l], b_merge[l].reshape(1, N_BRANCH * D_MODEL), wb[l], wo[l], norm_mix_post[l][None]],
            [D_MODEL], [F32])
        (h,) = _row_call(
            _ffn_kernel, "swiglu", batch, lt, [h],
            [norm_ffn_pre[l][None], wfg[l], wfu[l], wfd[l], norm_ffn_post[l][None]],
            [D_MODEL], [F32], scratch=[pltpu.VMEM((tm, D_MODEL), F32)])

    return h.reshape(batch, lt, d)[:, BLOCK:, :]
```

```python
import functools

import jax
import jax.numpy as jnp
from jax import lax
from jax.experimental import pallas as pl
from jax.experimental.pallas import tpu as pltpu

F32 = jnp.float32
BF16 = jnp.bfloat16

D_MODEL = 1024
BLOCK = 128
HEAD_DIM = 64
N_HEADS = 4
BW = N_HEADS * HEAD_DIM
N_BRANCH = 4
N_META = 16
N_FRONT_PAD = BLOCK - N_META
GLA_DK = 32
GLA_LOW_RANK = 16
GLA_GATE_NORMALIZER = 16.0
CONV_WIDTH = 4
D_FF = 2816
FF_CHUNK = 256
ROPE_BASE = 10000.0
NORM_EPS = 1e-6
SB_LOG_ZERO = -104.0
SB_STATIC_TILES = 3
V7X_VMEM_LIMIT = 56 * 1024 * 1024

CODE_LANE = 0
I_LANE = GLA_LOW_RANK
F_LANE = GLA_LOW_RANK + N_HEADS

_C = {}
_off = 0
for _name, _w in (("sb", 768), ("gla_q", 128), ("gla_k", 128), ("gla_v", 256), ("gla_r", 256), ("gla_code", 16),
                  ("ml_qk", 512), ("ml_v", 256), ("ml_i", 4), ("ml_f", 4), ("ml_o", 256), ("ret", 1024),
                  ("merge", 4096)):
    _C[_name] = (_off, _off + _w)
    _off += _w
IN_WIDTH = _off

U_SB, U_GLA, U_ML, U_RET = 768, 896, 1152, 1024


def _mm(a, b):
    return jnp.dot(a, b, preferred_element_type=F32)


def _mm_nt(a, b):
    return lax.dot_general(a, b, (((1,), (1,)), ((), ())), preferred_element_type=F32)


def _split(x):
    hi = x.astype(BF16)
    lo = (x - hi.astype(F32)).astype(BF16)
    return hi, lo


def _mm_split_r(c2, x):
    hi, lo = _split(x)
    return _mm(c2, jnp.concatenate([hi, lo], axis=0))


def _softplus(x):
    return jnp.maximum(x, 0.0) + jnp.log(1.0 + jnp.exp(-jnp.abs(x)))


def _log_sigmoid(x):
    return -_softplus(-x)


def _sigmoid(x):
    return 1.0 / (1.0 + jnp.exp(-x))


def _rmsnorm(x, g):
    return x * lax.rsqrt(jnp.mean(x * x, axis=-1, keepdims=True) + NORM_EPS) * g


def _head_rmsnorm(y, g, bdmean):
    ms = _mm((y * y).astype(BF16), bdmean)
    return y * lax.rsqrt(ms + NORM_EPS) * g


def _lanes_in(width, lo, hi):
    lane = lax.broadcasted_iota(jnp.int32, (1, width), 1)
    return (lane >= lo) & (lane < hi)


def _row_stack_heads(x, per_head, n=N_HEADS):
    keep = [_lanes_in(x.shape[1], per_head * h, per_head * (h + 1)).astype(x.dtype) for h in range(n)]
    return jnp.concatenate([x * m for m in keep], axis=0)


def _lane_cat_heads(s):
    return jnp.concatenate([s[BLOCK * h:BLOCK * (h + 1)] for h in range(N_HEADS)], axis=1)


def _tile4(x):
    return jnp.concatenate([x] * N_HEADS, axis=0)


def _rotary(x, cos, sin_signed):
    first_half = _lanes_in(BLOCK, 0, 32) | _lanes_in(BLOCK, 64, 96)
    outs = []
    for t in range(2):
        lanes = slice(BLOCK * t, BLOCK * (t + 1))
        xt = x[:, lanes]
        swapped = jnp.where(first_half, pltpu.roll(xt, 96, 1), pltpu.roll(xt, 32, 1))
        outs.append(xt * cos[:, lanes] + swapped * sin_signed[:, lanes])
    return jnp.concatenate(outs, axis=1)


def _proj_kernel(h_ref, cos_ref, sin_ref, g_ref, wsb_ref, wgla_ref, wml_ref, wret_ref, wsmall_ref, wg_ref, bg_ref,
                 bgate_ref, hn_ref, usb_ref, ugla_ref, uml_ref, uret_ref, *, tm, seq):
    x = h_ref[...]
    pos = pl.program_id(1) * tm + lax.broadcasted_iota(jnp.int32, (tm, 1), 0)
    valid = ((pos < seq) | (pos >= seq + N_FRONT_PAD)).astype(F32)
    hn = (_rmsnorm(x, g_ref[...]) * valid).astype(BF16)
    hn_ref[...] = hn
    usb_ref[...] = _mm(hn, wsb_ref[...]).astype(BF16)
    small = _mm(hn, wsmall_ref[...])
    ugla_ref[:, 0:768] = _mm(hn, wgla_ref[...])
    ugla_ref[:, 768:896] = (_log_sigmoid(_mm(small.astype(BF16), wg_ref[...]) + bg_ref[...])
                            * (1.0 / GLA_GATE_NORMALIZER))
    uml_ref[:, 0:1024] = _mm(hn, wml_ref[...])
    gates = small + bgate_ref[...]
    uml_ref[:, 1024:1152] = jnp.where(_lanes_in(BLOCK, F_LANE, F_LANE + N_HEADS), _log_sigmoid(gates), gates)
    ret = _mm(hn, wret_ref[...])
    cos = cos_ref[...]
    sin = sin_ref[...]
    uret_ref[:, 0:256] = _rotary(ret[:, 0:256], cos, sin)
    uret_ref[:, 256:512] = _rotary(ret[:, 256:512], cos, sin) * (HEAD_DIM ** -0.5)
    uret_ref[:, 512:1024] = ret[:, 512:1024]


def _sb_tile(q, k_ref, v_ref, j, n_blocks, bias, nuo2, cb):
    state = {"cb": cb, "out": None}
    for _ in _sb_tile_stages(q, k_ref, v_ref, j, n_blocks, bias, nuo2, state):
        pass
    return state["out"], state["cb"]


def _sb_tile_stages(q, k_ref, v_ref, j, n_blocks, bias, nuo2, state):
    start = pl.multiple_of(jnp.where(j <= 0, n_blocks - 1, j - 1) * BLOCK, BLOCK)
    kj = k_ref[pl.ds(start, BLOCK), :]
    zs = []
    for p in range(N_HEADS // 2):
        lanes = slice(BLOCK * p, BLOCK * (p + 1))
        k_rows = _row_stack_heads(kj[:, lanes], HEAD_DIM, 2)
        zs.append(_mm_nt(q[:, lanes], k_rows))
    yield
    z = jnp.concatenate(zs, axis=1) + bias
    hi, lo = _split(_softplus(z))
    rs = []
    for h in range(N_HEADS):
        lanes = slice(BLOCK * h, BLOCK * (h + 1))
        rs.append(_mm(jnp.concatenate([hi[:, lanes], lo[:, lanes]], axis=1), nuo2))
    yield
    cb = state["cb"]
    a = jnp.exp(z + jnp.concatenate([r[:, :BLOCK] for r in rs], axis=1) + cb).astype(BF16)
    state["cb"] = cb + jnp.concatenate([r[:, BLOCK:] for r in rs], axis=1)
    contrib = _mm(a, _row_stack_heads(v_ref[pl.ds(start, BLOCK), :], HEAD_DIM))
    yield
    state["out"] = contrib if state["out"] is None else state["out"] + contrib
    yield


def _sb_kernel(q_ref, k_ref, v_ref, nuo_ref, o_ref, acc_ref, cb_ref):
    i = pl.program_id(1)
    n_blocks = pl.num_programs(1)
    q = q_ref[...] * jnp.asarray(HEAD_DIM ** -0.5, BF16)
    row = lax.broadcasted_iota(jnp.int32, (BLOCK, BLOCK), 0)
    col = lax.broadcasted_iota(jnp.int32, (BLOCK, BLOCK), 1)
    nuo2 = nuo_ref[...]

    def bias_for(j, causal):
        ok = (j * BLOCK + col) >= N_FRONT_PAD
        if causal:
            ok = ok & (col < row)
        return jnp.concatenate([jnp.where(ok, 0.0, -jnp.inf)] * N_HEADS, axis=1)

    def all_max(x):
        return jnp.max(jnp.max(x, axis=1, keepdims=True), axis=0, keepdims=True)[0, 0]

    state = {"cb": jnp.zeros((BLOCK, N_HEADS * BLOCK), F32), "out": None}
    tiles = [_sb_tile_stages(q, k_ref, v_ref, i - t, n_blocks, bias_for(i - t, t == 0), nuo2, state)
             for t in range(SB_STATIC_TILES)]
    for _ in range(4):
        for tile in tiles:
            next(tile)
    cb = state["cb"]
    acc_ref[...] = state["out"]
    cb_ref[...] = cb

    def cond(c):
        j, mx = c
        return jnp.logical_and(j >= 0, mx > SB_LOG_ZERO)

    def body(c):
        j, _ = c
        contrib, cbn = _sb_tile(q, k_ref, v_ref, j, n_blocks, bias_for(j, False), nuo2, cb_ref[...])
        acc_ref[...] += contrib
        cb_ref[...] = cbn
        return j - 1, all_max(cbn)

    lax.while_loop(cond, body, (i - SB_STATIC_TILES, all_max(cb)))
    o_ref[...] = acc_ref[...]


def _gla_chunk(u_ref, tril_ref, bd_ref, o_ref, st_ref):
    q = u_ref[:, 0:128] * (GLA_DK ** -0.5)
    k = u_ref[:, 128:256]
    v = u_ref[:, 256:512]
    log_a = u_ref[:, 768:896]
    cum = _mm_split_r(tril_ref[...], log_a)
    yield
    cum_last = cum[BLOCK - 1:BLOCK, :]
    qd = (q * jnp.exp(cum)).astype(BF16)
    kd = (k * jnp.exp(-cum)).astype(BF16)
    ke = (k * jnp.exp(cum_last - cum)).astype(BF16)
    st = st_ref[...]
    inter = _mm_nt(qd, st.astype(BF16))
    scores = _mm_nt(_row_stack_heads(qd, GLA_DK), kd)
    local = _mm(v.T.astype(BF16), ke)
    yield
    row = lax.broadcasted_iota(jnp.int32, (BLOCK, BLOCK), 0)
    col = lax.broadcasted_iota(jnp.int32, (BLOCK, BLOCK), 1)
    s = jnp.where(_tile4(row >= col), scores, 0.0)
    intra = _mm(_lane_cat_heads(s.astype(BF16)), _row_stack_heads(v.astype(BF16), HEAD_DIM))
    st_ref[...] = st * jnp.exp(cum_last) + local * bd_ref[...]
    yield
    o_ref[...] = intra + inter


def _ret_chunk(u_ref, dec_ref, qdec_ref, kdec_ref, gam_ref, bd_ref, o_ref, st_ref):
    qr = u_ref[:, 0:256]
    kr = u_ref[:, 256:512]
    v = u_ref[:, 512:768]
    scores = _mm_nt(_row_stack_heads(qr.astype(BF16), HEAD_DIM), kr.astype(BF16))
    st = st_ref[...]
    inter = _mm_nt((qr * qdec_ref[...]).astype(BF16), st.astype(BF16))
    local = _mm(v.T.astype(BF16), (kr * kdec_ref[...]).astype(BF16))
    yield
    s = scores * dec_ref[...]
    intra = _mm(_lane_cat_heads(s.astype(BF16)), _row_stack_heads(v.astype(BF16), HEAD_DIM))
    st_ref[...] = st * gam_ref[...] + local * bd_ref[...]
    yield
    o_ref[...] = intra + inter


def _spread_head(x_t, h):
    own = _lanes_in(BLOCK, 0, HEAD_DIM) if h % 2 == 0 else _lanes_in(BLOCK, HEAD_DIM, BLOCK)
    return jnp.where(own, x_t, pltpu.roll(x_t, HEAD_DIM, 1))


def _ml_chunk(u_ref, cw_ref, tril_ref, e3_ref, bd_ref, bdones_ref, onestack_ref, o_ref, xbuf_ref, ct_ref, n_ref,
              m_ref):
    xbuf_ref[8:8 + BLOCK, :] = u_ref[:, 0:512]
    conv = None
    for j in range(CONV_WIDTH):
        term = cw_ref[j:j + 1, :] * xbuf_ref[8 - (CONV_WIDTH - 1) + j:8 - (CONV_WIDTH - 1) + j + BLOCK, :]
        conv = term if conv is None else conv + term
    xbuf_ref[0:8, :] = xbuf_ref[BLOCK:BLOCK + 8, :]
    qk = conv * _sigmoid(conv)
    q = qk[:, 0:BW]
    k = qk[:, BW:2 * BW] * (HEAD_DIM ** -0.5)
    v = u_ref[:, 512:768]

    gates = u_ref[:, 1024:1152]
    gates_cum = _mm_split_r(tril_ref[...], gates)
    q_b = q.astype(BF16)
    k_b = k.astype(BF16)
    qk_all = _mm_nt(_row_stack_heads(q_b, HEAD_DIM), k_b)
    yield
    compact = jnp.where(_lanes_in(BLOCK, F_LANE, F_LANE + N_HEADS), gates_cum, gates)
    t1 = compact.astype(BF16)
    rest = compact - t1.astype(F32)
    t2 = rest.astype(BF16)
    t3 = (rest - t2.astype(F32)).astype(BF16)
    spread = _mm(jnp.concatenate([t1, t2, t3], axis=1), e3_ref[...])
    m_s = m_ref[0:1, :]
    n_s = n_ref[0:1, :]
    ct = ct_ref[...]
    q_ct = _mm_nt(q_b, ct.astype(BF16))
    q_n = _mm((q * n_s).astype(BF16), bdones_ref[...])
    yield
    log_i = spread[:, 0:BW]
    cum = spread[:, BW:2 * BW]

    cum_last = cum[BLOCK - 1:BLOCK, :]
    w_end = cum_last - cum + log_i
    a_end = jnp.max(w_end, axis=0, keepdims=True)
    kp = k * jnp.exp(w_end - a_end)
    v_b = v.astype(BF16)
    c_loc = _mm(v.T.astype(BF16), kp.astype(BF16)) * bd_ref[...]
    n_loc = jnp.sum(kp, axis=0, keepdims=True)

    inter_log = cum + m_s
    d_t = (log_i - cum).T
    row = lax.broadcasted_iota(jnp.int32, (BLOCK, BLOCK), 0)
    col = lax.broadcasted_iota(jnp.int32, (BLOCK, BLOCK), 1)
    s_parts, m_parts = [], []
    for h in range(N_HEADS):
        t = h // 2
        cq = _spread_head(cum[:, BLOCK * t:BLOCK * (t + 1)], h)
        ilq = _spread_head(inter_log[:, BLOCK * t:BLOCK * (t + 1)], h)
        dk = jnp.broadcast_to(d_t[HEAD_DIM * h:HEAD_DIM * h + 1, :], (BLOCK, BLOCK))
        intra_log = jnp.where(row >= col, cq + dk, -jnp.inf)
        m_h = jnp.maximum(ilq, jnp.max(intra_log, axis=1, keepdims=True))
        s_parts.append(qk_all[BLOCK * h:BLOCK * (h + 1)] * jnp.exp(intra_log - m_h))
        m_parts.append(m_h)
    first = _lanes_in(BLOCK, 0, HEAD_DIM)
    m_t = jnp.concatenate([jnp.where(first, m_parts[0], m_parts[1]),
                           jnp.where(first, m_parts[2], m_parts[3])], axis=1)
    w_inter = jnp.exp(inter_log - m_t)
    s_b = jnp.concatenate(s_parts, axis=1).astype(BF16)
    nd = _mm(s_b, jnp.concatenate([_row_stack_heads(v_b, HEAD_DIM), onestack_ref[...]], axis=1))
    m_new = jnp.maximum(cum_last + m_s, a_end)
    s_prev = jnp.exp(cum_last + m_s - m_new)
    s_loc = jnp.exp(a_end - m_new)
    ct_ref[...] = s_prev * ct + s_loc * c_loc
    n_ref[...] = jnp.broadcast_to(s_prev * n_s + s_loc * n_loc, n_ref.shape)
    m_ref[...] = jnp.broadcast_to(m_new, m_ref.shape)
    yield
    num = nd[:, 0:BW] + w_inter * q_ct
    den = nd[:, BW:2 * BW] + w_inter * q_n
    o_ref[...] = num / jnp.maximum(jnp.abs(den), jnp.exp(-m_t))


def _recurrent_kernel(ugla_ref, uml_ref, uret_ref, cw_ref, tril_ref, e3_ref, bdgla_ref, bd_ref, bdones_ref,
                      onestack_ref, dec_ref, qdec_ref, kdec_ref, gam_ref,
                      ogla_ref, oml_ref, oret_ref,
                      stgla_ref, xbuf_ref, ct_ref, n_ref, m_ref, stret_ref):
    @pl.when(pl.program_id(0) == 0)
    def _():
        for ref in (stgla_ref, xbuf_ref, ct_ref, n_ref, m_ref, stret_ref):
            ref[...] = jnp.zeros_like(ref)

    chains = []
    for b in range(ugla_ref.shape[0]):
        chains += [
            _ml_chunk(uml_ref.at[b], cw_ref, tril_ref, e3_ref, bd_ref, bdones_ref, onestack_ref,
                      oml_ref.at[b], xbuf_ref.at[b], ct_ref.at[b], n_ref.at[b], m_ref.at[b]),
            _gla_chunk(ugla_ref.at[b], tril_ref, bdgla_ref, ogla_ref.at[b], stgla_ref.at[b]),
            _ret_chunk(uret_ref.at[b], dec_ref, qdec_ref, kdec_ref, gam_ref, bd_ref, oret_ref.at[b],
                       stret_ref.at[b])]
    while chains:
        chains = [chain for chain in chains if next(chain, True) is None]


def _merge_kernel(hn_ref, h_ref, ysb_ref, ogla_ref, oml_ref, oret_ref, r_ref, opre_ref, gret_ref,
                  wm_ref, bm_ref, wb_ref, wo_ref, g_ref, ngla_ref, nml_ref, nret_ref, bdmean_ref, o_ref):
    hn = hn_ref[...]
    bdmean = bdmean_ref[...]
    r = r_ref[...]
    g = gret_ref[...]
    branches = (ysb_ref[...],
                _head_rmsnorm(ogla_ref[...], ngla_ref[...], bdmean) * (r * _sigmoid(r)),
                _sigmoid(opre_ref[...]) * _head_rmsnorm(oml_ref[...], nml_ref[...], bdmean),
                (g * _sigmoid(g)) * _head_rmsnorm(oret_ref[...], nret_ref[...], bdmean))
    merged = None
    for n, y in enumerate(branches):
        cols = slice(D_MODEL * n, D_MODEL * (n + 1))
        gate = _sigmoid(_mm(hn, wm_ref[:, cols]) + bm_ref[:, cols])
        term = gate * _mm(y.astype(BF16), wb_ref[n])
        merged = term if merged is None else merged + term
    mix_out = _mm(merged.astype(BF16), wo_ref[...])
    o_ref[...] = h_ref[...] + _rmsnorm(mix_out, g_ref[...])


def _ffn_kernel(h_ref, gpre_ref, wg_ref, wu_ref, wd_ref, gpost_ref, o_ref, acc_ref):
    x = h_ref[...]
    f = _rmsnorm(x, gpre_ref[...]).astype(BF16)
    for c in range(D_FF // FF_CHUNK):
        cols = slice(FF_CHUNK * c, FF_CHUNK * (c + 1))
        a = _mm(f, wg_ref[:, cols])
        act = ((a * _sigmoid(a)) * _mm(f, wu_ref[:, cols])).astype(BF16)
        part = _mm(act, wd_ref[cols, :])
        if c == 0:
            acc_ref[...] = part
        else:
            acc_ref[...] += part
    o_ref[...] = x + _rmsnorm(acc_ref[...], gpost_ref[...])


def _largest_tile(n, candidates):
    for tm in candidates:
        if n % tm == 0:
            return tm
    raise ValueError(f"{n} rows are not a multiple of {BLOCK}")


def _const_spec(shape, grid_rank):
    zeros = (0,) * len(shape)
    return pl.BlockSpec(shape, (lambda b, j: zeros) if grid_rank == 2 else (lambda n: zeros))


def _params(sem):
    return pltpu.CompilerParams(dimension_semantics=sem, vmem_limit_bytes=V7X_VMEM_LIMIT)


def _row_call(kernel, name, batch, lt, row_inputs, tile_inputs, const_inputs, out_widths, out_dtypes):
    tm = _largest_tile(lt, (640, 512, 384, 256, 128))
    nb = lt // tm
    rows = batch * lt
    row_spec = lambda w, cb: pl.BlockSpec((tm, w), lambda b, j: (b * nb + j, cb))
    return pl.pallas_call(
        kernel,
        grid=(batch, nb),
        in_specs=([row_spec(w, cb) for _, w, cb in row_inputs]
                  + [pl.BlockSpec((tm, a.shape[1]), lambda b, j: (j, 0)) for a in tile_inputs]
                  + [_const_spec(a.shape, 2) for a in const_inputs]),
        out_specs=[row_spec(w, 0) for w in out_widths],
        out_shape=[jax.ShapeDtypeStruct((rows, w), dt) for w, dt in zip(out_widths, out_dtypes)],
        compiler_params=_params(("parallel", "parallel")),
        name=name,
    )(*[a for a, _, _ in row_inputs], *tile_inputs, *const_inputs)


def _full(a):
    return (a, a.shape[1], 0)


def _ffn_call(h, batch, lt, consts, final_seq=None):
    if final_seq is None:
        tm, rows_per_batch, out_rows = _largest_tile(lt, (640, 512, 384, 256, 128)), lt, lt
    else:
        tm, rows_per_batch, out_rows = _largest_tile(final_seq, (512, 256, 128)), final_seq, final_seq
    spec = pl.BlockSpec((None, tm, D_MODEL), lambda b, j: (b, j, 0))
    return pl.pallas_call(
        _ffn_kernel,
        grid=(batch, rows_per_batch // tm),
        in_specs=[spec] + [_const_spec(a.shape, 2) for a in consts],
        out_specs=spec,
        out_shape=jax.ShapeDtypeStruct((batch, out_rows, D_MODEL), F32),
        scratch_shapes=[pltpu.VMEM((tm, D_MODEL), F32)],
        compiler_params=_params(("parallel", "parallel")),
        name="swiglu",
    )(h.reshape(batch, lt, D_MODEL), *consts)


def _logical_to_memory_block(n, nc):
    return jnp.where(n == 0, nc - 1, n - 1)


def _recurrent_call(batch, lt, u_list, const_inputs):
    nc = lt // BLOCK
    u3 = [u.reshape(batch, lt, u.shape[1]) for u in u_list]
    chunk_map = lambda n: (0, _logical_to_memory_block(n, nc), 0)
    o_spec = pl.BlockSpec((batch, BLOCK, BW), chunk_map)
    outs = pl.pallas_call(
        _recurrent_kernel,
        grid=(nc,),
        in_specs=([pl.BlockSpec((batch, BLOCK, u.shape[2]), chunk_map) for u in u3]
                  + [_const_spec(a.shape, 1) for a in const_inputs]),
        out_specs=[o_spec] * 3,
        out_shape=[jax.ShapeDtypeStruct((batch, lt, BW), F32)] * 3,
        scratch_shapes=[pltpu.VMEM((batch, BW, BLOCK), F32),
                        pltpu.VMEM((batch, BLOCK + 8, 2 * BW), F32),
                        pltpu.VMEM((batch, BW, BW), F32),
                        pltpu.VMEM((batch, 8, BW), F32),
                        pltpu.VMEM((batch, 8, BW), F32),
                        pltpu.VMEM((batch, BW, BW), F32)],
        compiler_params=_params(("arbitrary",)),
        name="recurrent_mixers",
    )(*u3, *const_inputs)
    return [o.reshape(batch * lt, BW) for o in outs]


def _sb_call(u_sb, nuo, batch, lt):
    nc = lt // BLOCK
    q_map = lambda b, i: (b * nc + _logical_to_memory_block(i, nc), 0)
    return pl.pallas_call(
        _sb_kernel,
        grid=(batch, nc),
        in_specs=[pl.BlockSpec((BLOCK, BW), q_map),
                  pl.BlockSpec((lt, BW), lambda b, i: (b, 1)),
                  pl.BlockSpec((lt, BW), lambda b, i: (b, 2)),
                  _const_spec(nuo.shape, 2)],
        out_specs=pl.BlockSpec((BLOCK, BW), q_map),
        out_shape=jax.ShapeDtypeStruct((batch * lt, BW), F32),
        scratch_shapes=[pltpu.VMEM((BLOCK, BW), F32), pltpu.VMEM((BLOCK, N_HEADS * BLOCK), F32)],
        compiler_params=_params(("parallel", "parallel")),
        name="stick_breaking",
    )(u_sb, u_sb, u_sb, nuo)


def _constants(seq):
    idx = jnp.arange(BLOCK)
    tril = (idx[:, None] >= idx[None, :])
    nuo = -jnp.concatenate([tril, jnp.ones((BLOCK, BLOCK), bool)], axis=1).astype(BF16)
    c = {"tril": jnp.concatenate([tril, tril], axis=1).astype(BF16),
         "nuo": jnp.concatenate([nuo, nuo], axis=0)}
    head_e = jnp.arange(BW) // HEAD_DIM
    c["bd_gla"] = (head_e[:, None] == (jnp.arange(BLOCK) // GLA_DK)[None, :]).astype(F32)
    bd = head_e[:, None] == head_e[None, :]
    c["bd"] = bd.astype(F32)
    c["bdones"] = bd.astype(BF16)
    c["bdmean"] = (bd.astype(F32) / HEAD_DIM).astype(BF16)
    c["onestack"] = ((jnp.arange(N_HEADS * BLOCK) // BLOCK)[:, None] == head_e[None, :]).astype(BF16)
    lane = jnp.arange(BLOCK)[:, None]
    spread = jnp.concatenate([lane == I_LANE + head_e[None, :], lane == F_LANE + head_e[None, :]], axis=1)
    c["e3"] = jnp.concatenate([spread] * 3, axis=0).astype(BF16)
    positions = jnp.concatenate([jnp.arange(N_META, seq + N_META), jnp.arange(-N_FRONT_PAD, N_META)])
    half = HEAD_DIM // 2
    inv_freq = ROPE_BASE ** (-jnp.arange(half, dtype=F32) / half)
    ang = positions.astype(jnp.int32).astype(F32)[:, None] * inv_freq[None, :]
    cos, sin = jnp.cos(ang), jnp.sin(ang)
    c["cos"] = jnp.tile(jnp.concatenate([cos, cos], axis=1), (1, N_HEADS))
    c["sin"] = jnp.tile(jnp.concatenate([-sin, sin], axis=1), (1, N_HEADS))
    log_gamma = jnp.log1p(-jnp.exp2(-5.0 - jnp.arange(N_HEADS, dtype=F32)))
    pos = jnp.arange(BLOCK, dtype=F32)
    diff = jnp.maximum(pos[:, None] - pos[None, :], 0.0)
    c["ret_dec"] = jnp.where(tril, jnp.exp(log_gamma[:, None, None] * diff), 0.0).reshape(N_HEADS * BLOCK, BLOCK)
    c["ret_qdec"] = jnp.repeat(jnp.exp(log_gamma[:, None] * (pos + 1.0)).T, HEAD_DIM, axis=1)
    c["ret_kdec"] = jnp.repeat(jnp.exp(log_gamma[:, None] * (BLOCK - 1.0 - pos)).T, HEAD_DIM, axis=1)
    c["ret_gam"] = jnp.repeat(jnp.exp(log_gamma * BLOCK), HEAD_DIM)[None, :]
    return c


def _cols(w, name):
    a, b = _C[name]
    return w[..., a:b]


def _layout_weights(w_in):
    w_in = w_in.astype(BF16)
    pad = jnp.zeros(w_in.shape[:-1] + (BLOCK - GLA_LOW_RANK - 2 * N_HEADS,), BF16)
    return {
        "sb": _cols(w_in, "sb"),
        "gla": jnp.concatenate([_cols(w_in, n) for n in ("gla_q", "gla_k", "gla_v", "gla_r")], axis=-1),
        "ml": jnp.concatenate([_cols(w_in, n) for n in ("ml_qk", "ml_v", "ml_o")], axis=-1),
        "ret": _cols(w_in, "ret"),
        "small": jnp.concatenate([_cols(w_in, "gla_code"), _cols(w_in, "ml_i"), _cols(w_in, "ml_f"), pad], axis=-1),
        "merge": _cols(w_in, "merge"),
    }


def kernel(x, meta_tokens, norm_mix_pre, norm_mix_post, norm_ffn_pre, norm_ffn_post, w_in,
           gla_w_gate_up, gla_b_gate, gla_norm, ml_conv, ml_b_i, ml_b_f, ml_norm, ret_norm,
           w_branch, b_merge, w_out, ffn_w_gate, ffn_w_up, ffn_w_down):
    batch, seq, d = x.shape
    depth = w_in.shape[0]
    lt = seq + BLOCK
    pad = jnp.zeros((batch, N_FRONT_PAD, d), x.dtype)
    meta = jnp.broadcast_to(meta_tokens.astype(x.dtype)[None], (batch, N_META, d))
    h = jnp.concatenate([x, pad, meta], axis=1).reshape(batch * lt, d)

    c = _constants(seq)
    w = _layout_weights(w_in)
    wg_gla = jnp.pad(gla_w_gate_up, ((0, 0), (0, BLOCK - GLA_LOW_RANK), (0, 0))).astype(BF16)
    tail = BLOCK - GLA_LOW_RANK - 2 * N_HEADS
    b_gate = jnp.pad(jnp.concatenate([ml_b_i, ml_b_f], axis=-1), ((0, 0), (GLA_LOW_RANK, tail)))
    wb = w_branch.astype(BF16)
    wo = w_out.astype(BF16)
    wfg, wfu, wfd = ffn_w_gate.astype(BF16), ffn_w_up.astype(BF16), ffn_w_down.astype(BF16)
    tm = _largest_tile(lt, (640, 512, 384, 256, 128))

    for l in range(depth):
        hn, u_sb, u_gla, u_ml, u_ret = _row_call(
            functools.partial(_proj_kernel, tm=tm, seq=seq), "in_projection", batch, lt,
            [_full(h)], [c["cos"], c["sin"]],
            [norm_mix_pre[l][None], w["sb"][l], w["gla"][l], w["ml"][l], w["ret"][l], w["small"][l], wg_gla[l],
             gla_b_gate[l][None], b_gate[l][None]],
            [D_MODEL, U_SB, U_GLA, U_ML, U_RET], [BF16, BF16, F32, F32, F32])
        y_sb = _sb_call(u_sb, c["nuo"], batch, lt)
        o_gla, o_ml, o_ret = _recurrent_call(
            batch, lt, [u_gla, u_ml, u_ret],
            [ml_conv[l], c["tril"], c["e3"], c["bd_gla"], c["bd"], c["bdones"], c["onestack"],
             c["ret_dec"], c["ret_qdec"], c["ret_kdec"], c["ret_gam"]])
        (h,) = _row_call(
            _merge_kernel, "merge", batch, lt,
            [_full(hn), _full(h), _full(y_sb), _full(o_gla), _full(o_ml), _full(o_ret),
             (u_gla, BW, 2), (u_ml, BW, 3), (u_ret, BW, 3)], [],
            [w["merge"][l], b_merge[l].reshape(1, N_BRANCH * D_MODEL), wb[l], wo[l], norm_mix_post[l][None],
             gla_norm[l][None], ml_norm[l][None], ret_norm[l][None], c["bdmean"]],
            [D_MODEL], [F32])
        ffn_consts = [norm_ffn_pre[l][None], wfg[l], wfu[l], wfd[l], norm_ffn_post[l][None]]
        if l + 1 < depth:
            h = _ffn_call(h, batch, lt, ffn_consts).reshape(batch * lt, d)
        else:
            return _ffn_call(h, batch, lt, ffn_consts, final_seq=seq)
```

```python
import functools

import jax
import jax.numpy as jnp
from jax import lax
from jax.experimental import pallas as pl
from jax.experimental.pallas import tpu as pltpu

F32 = jnp.float32
BF16 = jnp.bfloat16

D_MODEL = 1024
BLOCK = 128
HEAD_DIM = 64
N_HEADS = 4
BW = N_HEADS * HEAD_DIM
N_BRANCH = 4
N_META = 16
N_FRONT_PAD = BLOCK - N_META
GLA_DK = 32
GLA_LOW_RANK = 16
GLA_GATE_NORMALIZER = 16.0
CONV_WIDTH = 4
D_FF = 2816
FF_CHUNK = 256
ROPE_BASE = 10000.0
NORM_EPS = 1e-6
SB_LOG_ZERO = -104.0
SB_STATIC_TILES = 3
V7X_VMEM_LIMIT = 56 * 1024 * 1024

CODE_LANE = 0
I_LANE = GLA_LOW_RANK
F_LANE = GLA_LOW_RANK + N_HEADS

_C = {}
_off = 0
for _name, _w in (("sb", 768), ("gla_q", 128), ("gla_k", 128), ("gla_v", 256), ("gla_r", 256), ("gla_code", 16),
                  ("ml_qk", 512), ("ml_v", 256), ("ml_i", 4), ("ml_f", 4), ("ml_o", 256), ("ret", 1024),
                  ("merge", 4096)):
    _C[_name] = (_off, _off + _w)
    _off += _w
IN_WIDTH = _off

U_SB, U_GLA, U_ML, U_RET = 768, 896, 1152, 1024


def _mm(a, b):
    return jnp.dot(a, b, preferred_element_type=F32)


def _mm_nt(a, b):
    return lax.dot_general(a, b, (((1,), (1,)), ((), ())), preferred_element_type=F32)


def _split(x):
    hi = x.astype(BF16)
    lo = (x - hi.astype(F32)).astype(BF16)
    return hi, lo


def _mm_split_r(c2, x):
    hi, lo = _split(x)
    return _mm(c2, jnp.concatenate([hi, lo], axis=0))


def _softplus(x):
    return jnp.maximum(x, 0.0) + jnp.log(1.0 + jnp.exp(-jnp.abs(x)))


def _log_sigmoid(x):
    return -_softplus(-x)


def _sigmoid(x):
    return 1.0 / (1.0 + jnp.exp(-x))


def _rmsnorm(x, g):
    return x * lax.rsqrt(jnp.mean(x * x, axis=-1, keepdims=True) + NORM_EPS) * g


def _head_rmsnorm(y, g, bdmean):
    ms = _mm((y * y).astype(BF16), bdmean)
    return y * lax.rsqrt(ms + NORM_EPS) * g


def _lanes_in(width, lo, hi):
    lane = lax.broadcasted_iota(jnp.int32, (1, width), 1)
    return (lane >= lo) & (lane < hi)


def _row_stack_heads(x, per_head, n=N_HEADS):
    keep = [_lanes_in(x.shape[1], per_head * h, per_head * (h + 1)).astype(x.dtype) for h in range(n)]
    return jnp.concatenate([x * m for m in keep], axis=0)


def _pair_scores(q_b, k_b):
    zs = []
    for p in range(N_HEADS // 2):
        lanes = slice(BLOCK * p, BLOCK * (p + 1))
        zs.append(_mm_nt(q_b[:, lanes], _row_stack_heads(k_b[:, lanes], HEAD_DIM, 2)))
    return jnp.concatenate(zs, axis=1)


def _pair_apply(s_b, v_b, extra=None):
    outs, extras = [], []
    for p in range(N_HEADS // 2):
        rhs = _row_stack_heads(v_b[:, BLOCK * p:BLOCK * (p + 1)], HEAD_DIM, 2)
        if extra is not None:
            rhs = jnp.concatenate([rhs, extra], axis=1)
        r = _mm(s_b[:, 2 * BLOCK * p:2 * BLOCK * (p + 1)], rhs)
        outs.append(r[:, :BLOCK])
        extras.append(r[:, BLOCK:])
    out = jnp.concatenate(outs, axis=1)
    return out if extra is None else (out, jnp.concatenate(extras, axis=1))


def _tile4(x):
    return jnp.concatenate([x] * N_HEADS, axis=1)


def _rotary(x, cos, sin_signed):
    first_half = _lanes_in(BLOCK, 0, 32) | _lanes_in(BLOCK, 64, 96)
    outs = []
    for t in range(2):
        lanes = slice(BLOCK * t, BLOCK * (t + 1))
        xt = x[:, lanes]
        swapped = jnp.where(first_half, pltpu.roll(xt, 96, 1), pltpu.roll(xt, 32, 1))
        outs.append(xt * cos[:, lanes] + swapped * sin_signed[:, lanes])
    return jnp.concatenate(outs, axis=1)


def _proj_kernel(h_ref, cos_ref, sin_ref, g_ref, wsb_ref, wgla_ref, wml_ref, wret_ref, wsmall_ref, wg_ref, bg_ref,
                 bgate_ref, hn_ref, usb_ref, ugla_ref, uml_ref, uret_ref, *, tm, seq):
    x = h_ref[...]
    pos = pl.program_id(1) * tm + lax.broadcasted_iota(jnp.int32, (tm, 1), 0)
    valid = ((pos < seq) | (pos >= seq + N_FRONT_PAD)).astype(F32)
    hn = (_rmsnorm(x, g_ref[...]) * valid).astype(BF16)
    hn_ref[...] = hn
    usb_ref[...] = _mm(hn, wsb_ref[...].astype(BF16)).astype(BF16)
    small = _mm(hn, wsmall_ref[...])
    ugla_ref[:, 0:768] = _mm(hn, wgla_ref[...].astype(BF16))
    ugla_ref[:, 768:896] = (_log_sigmoid(_mm(small.astype(BF16), wg_ref[...]) + bg_ref[...])
                            * (1.0 / GLA_GATE_NORMALIZER))
    uml_ref[:, 0:1024] = _mm(hn, wml_ref[...])
    gates = small + bgate_ref[...]
    uml_ref[:, 1024:1152] = jnp.where(_lanes_in(BLOCK, F_LANE, F_LANE + N_HEADS), _log_sigmoid(gates), gates)
    ret = _mm(hn, wret_ref[...])
    cos = cos_ref[...]
    sin = sin_ref[...]
    uret_ref[:, 0:256] = _rotary(ret[:, 0:256], cos, sin)
    uret_ref[:, 256:512] = _rotary(ret[:, 256:512], cos, sin) * (HEAD_DIM ** -0.5)
    uret_ref[:, 512:1024] = ret[:, 512:1024]


def _sb_tile(q, k_ref, v_ref, j, n_blocks, bias, nuo2, cb):
    state = {"cb": cb, "out": None}
    for _ in _sb_tile_stages(q, k_ref, v_ref, j, n_blocks, bias, nuo2, state):
        pass
    return state["out"], state["cb"]


def _sb_tile_stages(q, k_ref, v_ref, j, n_blocks, bias, nuo2, state):
    start = pl.multiple_of(jnp.where(j <= 0, n_blocks - 1, j - 1) * BLOCK, BLOCK)
    kj = k_ref[pl.ds(start, BLOCK), :]
    zs = []
    for p in range(N_HEADS // 2):
        lanes = slice(BLOCK * p, BLOCK * (p + 1))
        k_rows = _row_stack_heads(kj[:, lanes], HEAD_DIM, 2)
        zs.append(_mm_nt(q[:, lanes], k_rows))
    yield
    z = jnp.concatenate(zs, axis=1) + bias
    hi, lo = _split(_softplus(z))
    rs = []
    for h in range(N_HEADS):
        lanes = slice(BLOCK * h, BLOCK * (h + 1))
        rs.append(_mm(jnp.concatenate([hi[:, lanes], lo[:, lanes]], axis=1), nuo2))
    yield
    cb = state["cb"]
    a = jnp.exp(z + jnp.concatenate([r[:, :BLOCK] for r in rs], axis=1) + cb).astype(BF16)
    state["cb"] = cb + jnp.concatenate([r[:, BLOCK:] for r in rs], axis=1)
    contrib = _mm(a, _row_stack_heads(v_ref[pl.ds(start, BLOCK), :], HEAD_DIM))
    yield
    acc_ref = state.get("acc_ref")
    if acc_ref is None:
        state["out"] = contrib if state["out"] is None else state["out"] + contrib
    elif state["out"] is None:
        acc_ref[...] = contrib
        state["out"] = acc_ref
    else:
        acc_ref[...] += contrib
    yield


def _sb_kernel(q_ref, k_ref, v_ref, nuo_ref, o_ref, acc_ref, cb_ref):
    i = pl.program_id(0)
    n_blocks = pl.num_programs(0)
    batch = q_ref.shape[0]
    qs = [q_ref[b] * jnp.asarray(HEAD_DIM ** -0.5, BF16) for b in range(batch)]
    row = lax.broadcasted_iota(jnp.int32, (BLOCK, BLOCK), 0)
    col = lax.broadcasted_iota(jnp.int32, (BLOCK, BLOCK), 1)
    nuo2 = nuo_ref[...]

    def bias_for(j, causal):
        ok = (j * BLOCK + col) >= N_FRONT_PAD
        if causal:
            ok = ok & (col < row)
        return jnp.concatenate([jnp.where(ok, 0.0, -jnp.inf)] * N_HEADS, axis=1)

    def all_max(x):
        return jnp.max(jnp.max(x, axis=1, keepdims=True), axis=0, keepdims=True)[0, 0]

    biases = [bias_for(i - t, t == 0) for t in range(SB_STATIC_TILES)]
    states = [{"cb": jnp.zeros((BLOCK, N_HEADS * BLOCK), F32), "out": None, "acc_ref": acc_ref.at[b]}
              for b in range(batch)]
    tiles = [_sb_tile_stages(qs[b], k_ref.at[b], v_ref.at[b], i - t, n_blocks, biases[t], nuo2, states[b])
             for t in range(SB_STATIC_TILES) for b in range(batch)]
    for _ in range(4):
        for tile in tiles:
            next(tile)

    def cond(c):
        j, mx = c
        return jnp.logical_and(j >= 0, mx > SB_LOG_ZERO)

    for b in range(batch):
        cb_ref[b] = states[b]["cb"]

        def body(c, b=b):
            j, _ = c
            contrib, cbn = _sb_tile(qs[b], k_ref.at[b], v_ref.at[b], j, n_blocks, bias_for(j, False), nuo2,
                                    cb_ref[b])
            acc_ref[b] += contrib
            cb_ref[b] = cbn
            return j - 1, all_max(cbn)

        lax.while_loop(cond, body, (i - SB_STATIC_TILES, all_max(states[b]["cb"])))
        o_ref[b] = acc_ref[b]


def _gla_chunk(u_ref, tril_ref, bd_ref, o_ref, st_ref):
    q = u_ref[:, 0:128] * (GLA_DK ** -0.5)
    k = u_ref[:, 128:256]
    v = u_ref[:, 256:512]
    log_a = u_ref[:, 768:896]
    cum = _mm_split_r(tril_ref[...], log_a)
    yield
    cum_last = cum[BLOCK - 1:BLOCK, :]
    qd = (q * jnp.exp(cum)).astype(BF16)
    kd = (k * jnp.exp(-cum)).astype(BF16)
    ke = (k * jnp.exp(cum_last - cum)).astype(BF16)
    st = st_ref[...]
    inter = _mm_nt(qd, st.astype(BF16))
    scores = _mm_nt(qd, _row_stack_heads(kd, GLA_DK))
    local = _mm(v.T.astype(BF16), ke)
    yield
    row = lax.broadcasted_iota(jnp.int32, (BLOCK, BLOCK), 0)
    col = lax.broadcasted_iota(jnp.int32, (BLOCK, BLOCK), 1)
    s = jnp.where(_tile4(row >= col), scores, 0.0)
    intra = _pair_apply(s.astype(BF16), v.astype(BF16))
    yield
    st_ref[...] = st * jnp.exp(cum_last) + local * bd_ref[...]
    o_ref[...] = intra + inter


def _ret_chunk(u_ref, dec_ref, qdec_ref, kdec_ref, gam_ref, bd_ref, o_ref, st_ref):
    qr = u_ref[:, 0:256]
    kr = u_ref[:, 256:512]
    v = u_ref[:, 512:768]
    scores = _pair_scores(qr.astype(BF16), kr.astype(BF16))
    st = st_ref[...]
    inter = _mm_nt((qr * qdec_ref[...]).astype(BF16), st.astype(BF16))
    local = _mm(v.T.astype(BF16), (kr * kdec_ref[...]).astype(BF16))
    yield
    s = scores * dec_ref[...]
    intra = _pair_apply(s.astype(BF16), v.astype(BF16))
    yield
    st_ref[...] = st * gam_ref[...] + local * bd_ref[...]
    o_ref[...] = intra + inter


def _spread_head(x_t, h):
    own = _lanes_in(BLOCK, 0, HEAD_DIM) if h % 2 == 0 else _lanes_in(BLOCK, HEAD_DIM, BLOCK)
    return jnp.where(own, x_t, pltpu.roll(x_t, HEAD_DIM, 1))


def _ml_chunk(u_ref, cw_ref, tril_ref, e3_ref, bd_ref, bdones_ref, onestack_ref, o_ref, xbuf_ref, ct_ref, n_ref,
              m_ref):
    gates = u_ref[:, 1024:1152]
    gates_cum = _mm_split_r(tril_ref[...], gates)
    yield
    compact = jnp.where(_lanes_in(BLOCK, F_LANE, F_LANE + N_HEADS), gates_cum, gates)
    t1 = compact.astype(BF16)
    rest = compact - t1.astype(F32)
    t2 = rest.astype(BF16)
    t3 = (rest - t2.astype(F32)).astype(BF16)
    spread = _mm(jnp.concatenate([t1, t2, t3], axis=1), e3_ref[...])

    xbuf_ref[8:8 + BLOCK, :] = u_ref[:, 0:512]
    conv = None
    for j in range(CONV_WIDTH):
        term = cw_ref[j:j + 1, :] * xbuf_ref[8 - (CONV_WIDTH - 1) + j:8 - (CONV_WIDTH - 1) + j + BLOCK, :]
        conv = term if conv is None else conv + term
    xbuf_ref[0:8, :] = xbuf_ref[BLOCK:BLOCK + 8, :]
    qk = conv * _sigmoid(conv)
    q = qk[:, 0:BW]
    k = qk[:, BW:2 * BW] * (HEAD_DIM ** -0.5)
    v = u_ref[:, 512:768]
    q_b = q.astype(BF16)
    k_b = k.astype(BF16)
    qk_all = _pair_scores(q_b, k_b)
    m_s = m_ref[0:1, :]
    n_s = n_ref[0:1, :]
    ct = ct_ref[...]
    q_ct = _mm_nt(q_b, ct.astype(BF16))
    q_n = _mm((q * n_s).astype(BF16), bdones_ref[...])
    yield
    log_i = spread[:, 0:BW]
    cum = spread[:, BW:2 * BW]

    cum_last = cum[BLOCK - 1:BLOCK, :]
    w_end = cum_last - cum + log_i
    a_end = jnp.max(w_end, axis=0, keepdims=True)
    kp = k * jnp.exp(w_end - a_end)
    v_b = v.astype(BF16)
    c_loc = _mm(v.T.astype(BF16), kp.astype(BF16)) * bd_ref[...]
    n_loc = jnp.sum(kp, axis=0, keepdims=True)

    inter_log = cum + m_s
    d_t = (log_i - cum).T
    yield
    row = lax.broadcasted_iota(jnp.int32, (BLOCK, BLOCK), 0)
    col = lax.broadcasted_iota(jnp.int32, (BLOCK, BLOCK), 1)
    s_parts, m_parts = [], []
    for h in range(N_HEADS):
        if h == N_HEADS // 2:
            yield
        t = h // 2
        cq = _spread_head(cum[:, BLOCK * t:BLOCK * (t + 1)], h)
        ilq = _spread_head(inter_log[:, BLOCK * t:BLOCK * (t + 1)], h)
        dk = jnp.broadcast_to(d_t[HEAD_DIM * h:HEAD_DIM * h + 1, :], (BLOCK, BLOCK))
        intra_log = jnp.where(row >= col, cq + dk, -jnp.inf)
        m_h = jnp.maximum(ilq, jnp.max(intra_log, axis=1, keepdims=True))
        s_parts.append(qk_all[:, BLOCK * h:BLOCK * (h + 1)] * jnp.exp(intra_log - m_h))
        m_parts.append(m_h)
    first = _lanes_in(BLOCK, 0, HEAD_DIM)
    m_t = jnp.concatenate([jnp.where(first, m_parts[0], m_parts[1]),
                           jnp.where(first, m_parts[2], m_parts[3])], axis=1)
    w_inter = jnp.exp(inter_log - m_t)
    s_b = jnp.concatenate(s_parts, axis=1).astype(BF16)
    num_intra, den_intra = _pair_apply(s_b, v_b, extra=onestack_ref[...])
    yield
    m_new = jnp.maximum(cum_last + m_s, a_end)
    s_prev = jnp.exp(cum_last + m_s - m_new)
    s_loc = jnp.exp(a_end - m_new)
    ct_ref[...] = s_prev * ct + s_loc * c_loc
    n_ref[...] = jnp.broadcast_to(s_prev * n_s + s_loc * n_loc, n_ref.shape)
    m_ref[...] = jnp.broadcast_to(m_new, m_ref.shape)
    num = num_intra + w_inter * q_ct
    den = den_intra + w_inter * q_n
    o_ref[...] = num / jnp.maximum(jnp.abs(den), jnp.exp(-m_t))


def _recurrent_kernel(ugla_ref, uml_ref, uret_ref, cw_ref, tril_ref, e3_ref, bdgla_ref, bd_ref, bdones_ref,
                      onestack_ref, dec_ref, qdec_ref, kdec_ref, gam_ref,
                      ogla_ref, oml_ref, oret_ref,
                      stgla_ref, xbuf_ref, ct_ref, n_ref, m_ref, stret_ref):
    @pl.when(pl.program_id(0) == 0)
    def _():
        for ref in (stgla_ref, xbuf_ref, ct_ref, n_ref, m_ref, stret_ref):
            ref[...] = jnp.zeros_like(ref)

    chains = []
    for b in range(ugla_ref.shape[0]):
        chains += [
            _ml_chunk(uml_ref.at[b], cw_ref, tril_ref, e3_ref, bd_ref, bdones_ref, onestack_ref,
                      oml_ref.at[b], xbuf_ref.at[b], ct_ref.at[b], n_ref.at[b], m_ref.at[b]),
            _gla_chunk(ugla_ref.at[b], tril_ref, bdgla_ref, ogla_ref.at[b], stgla_ref.at[b]),
            _ret_chunk(uret_ref.at[b], dec_ref, qdec_ref, kdec_ref, gam_ref, bd_ref, oret_ref.at[b],
                       stret_ref.at[b])]
    while chains:
        chains = [chain for chain in chains if next(chain, True) is None]


def _merge_kernel(hn_ref, h_ref, ysb_ref, ogla_ref, oml_ref, oret_ref, r_ref, opre_ref, gret_ref,
                  wm_ref, bm_ref, wb_ref, wo_ref, g_ref, ngla_ref, nml_ref, nret_ref, bdmean_ref, o_ref):
    hn = hn_ref[...]
    bdmean = bdmean_ref[...]
    r = r_ref[...]
    g = gret_ref[...]
    branches = (ysb_ref[...],
                _head_rmsnorm(ogla_ref[...], ngla_ref[...], bdmean) * (r * _sigmoid(r)),
                _sigmoid(opre_ref[...]) * _head_rmsnorm(oml_ref[...], nml_ref[...], bdmean),
                (g * _sigmoid(g)) * _head_rmsnorm(oret_ref[...], nret_ref[...], bdmean))
    merged = None
    for n, y in enumerate(branches):
        cols = slice(D_MODEL * n, D_MODEL * (n + 1))
        gate = _sigmoid(_mm(hn, wm_ref[:, cols]) + bm_ref[:, cols])
        term = gate * _mm(y.astype(BF16), wb_ref[n])
        merged = term if merged is None else merged + term
    mix_out = _mm(merged.astype(BF16), wo_ref[...])
    o_ref[...] = h_ref[...] + _rmsnorm(mix_out, g_ref[...])


def _ffn_kernel(h_ref, gpre_ref, wg_ref, wu_ref, wd_ref, gpost_ref, o_ref, acc_ref):
    x = h_ref[...]
    f = _rmsnorm(x, gpre_ref[...]).astype(BF16)
    for c in range(D_FF // FF_CHUNK):
        cols = slice(FF_CHUNK * c, FF_CHUNK * (c + 1))
        a = _mm(f, wg_ref[:, cols])
        act = ((a * _sigmoid(a)) * _mm(f, wu_ref[:, cols])).astype(BF16)
        part = _mm(act, wd_ref[cols, :])
        if c == 0:
            acc_ref[...] = part
        else:
            acc_ref[...] += part
    o_ref[...] = x + _rmsnorm(acc_ref[...], gpost_ref[...])


def _largest_tile(n, candidates):
    for tm in candidates:
        if n % tm == 0:
            return tm
    raise ValueError(f"{n} rows are not a multiple of {BLOCK}")


def _const_spec(shape, grid_rank):
    zeros = (0,) * len(shape)
    return pl.BlockSpec(shape, (lambda b, j: zeros) if grid_rank == 2 else (lambda n: zeros))


def _params(sem):
    return pltpu.CompilerParams(dimension_semantics=sem, vmem_limit_bytes=V7X_VMEM_LIMIT)


def _row_call(kernel, name, batch, lt, row_inputs, tile_inputs, const_inputs, out_widths, out_dtypes):
    tm = _largest_tile(lt, (640, 512, 384, 256, 128))
    nb = lt // tm
    rows = batch * lt
    row_spec = lambda w, cb: pl.BlockSpec((tm, w), lambda b, j: (b * nb + j, cb))
    return pl.pallas_call(
        kernel,
        grid=(batch, nb),
        in_specs=([row_spec(w, cb) for _, w, cb in row_inputs]
                  + [pl.BlockSpec((tm, a.shape[1]), lambda b, j: (j, 0)) for a in tile_inputs]
                  + [a[1] if isinstance(a, tuple) else _const_spec(a.shape, 2) for a in const_inputs]),
        out_specs=[row_spec(w, 0) for w in out_widths],
        out_shape=[jax.ShapeDtypeStruct((rows, w), dt) for w, dt in zip(out_widths, out_dtypes)],
        compiler_params=_params(("parallel", "parallel")),
        name=name,
    )(*[a for a, _, _ in row_inputs], *tile_inputs, *[a[0] if isinstance(a, tuple) else a for a in const_inputs])


def _w_in_group(w_in, layer, width, index):
    return (w_in, pl.BlockSpec((None, D_MODEL, width), lambda b, j: (layer, 0, index)))


def _full(a):
    return (a, a.shape[1], 0)


def _ffn_call(h, batch, lt, consts, final_seq=None):
    if final_seq is None:
        tm, rows_per_batch, out_rows = _largest_tile(lt, (640, 512, 384, 256, 128)), lt, lt
    else:
        tm, rows_per_batch, out_rows = _largest_tile(final_seq, (512, 256, 128)), final_seq, final_seq
    spec = pl.BlockSpec((None, tm, D_MODEL), lambda b, j: (b, j, 0))
    return pl.pallas_call(
        _ffn_kernel,
        grid=(batch, rows_per_batch // tm),
        in_specs=[spec] + [_const_spec(a.shape, 2) for a in consts],
        out_specs=spec,
        out_shape=jax.ShapeDtypeStruct((batch, out_rows, D_MODEL), F32),
        scratch_shapes=[pltpu.VMEM((tm, D_MODEL), F32)],
        compiler_params=_params(("parallel", "parallel")),
        name="swiglu",
    )(h.reshape(batch, lt, D_MODEL), *consts)


def _logical_to_memory_block(n, nc):
    return jnp.where(n == 0, nc - 1, n - 1)


def _recurrent_call(batch, lt, u_list, const_inputs):
    nc = lt // BLOCK
    u3 = [u.reshape(batch, lt, u.shape[1]) for u in u_list]
    chunk_map = lambda n: (0, _logical_to_memory_block(n, nc), 0)
    o_spec = pl.BlockSpec((batch, BLOCK, BW), chunk_map)
    outs = pl.pallas_call(
        _recurrent_kernel,
        grid=(nc,),
        in_specs=([pl.BlockSpec((batch, BLOCK, u.shape[2]), chunk_map) for u in u3]
                  + [_const_spec(a.shape, 1) for a in const_inputs]),
        out_specs=[o_spec] * 3,
        out_shape=[jax.ShapeDtypeStruct((batch, lt, BW), F32)] * 3,
        scratch_shapes=[pltpu.VMEM((batch, BW, BLOCK), F32),
                        pltpu.VMEM((batch, BLOCK + 8, 2 * BW), F32),
                        pltpu.VMEM((batch, BW, BW), F32),
                        pltpu.VMEM((batch, 8, BW), F32),
                        pltpu.VMEM((batch, 8, BW), F32),
                        pltpu.VMEM((batch, BW, BW), F32)],
        compiler_params=_params(("arbitrary",)),
        name="recurrent_mixers",
    )(*u3, *const_inputs)
    return [o.reshape(batch * lt, BW) for o in outs]


def _sb_call(u_sb, nuo, batch, lt):
    nc = lt // BLOCK
    u3 = u_sb.reshape(batch, lt, U_SB)
    q_spec = pl.BlockSpec((batch, BLOCK, BW), lambda i: (0, _logical_to_memory_block(i, nc), 0))
    return pl.pallas_call(
        _sb_kernel,
        grid=(nc,),
        in_specs=[q_spec,
                  pl.BlockSpec((batch, lt, BW), lambda i: (0, 0, 1)),
                  pl.BlockSpec((batch, lt, BW), lambda i: (0, 0, 2)),
                  _const_spec(nuo.shape, 1)],
        out_specs=q_spec,
        out_shape=jax.ShapeDtypeStruct((batch, lt, BW), F32),
        scratch_shapes=[pltpu.VMEM((batch, BLOCK, BW), F32), pltpu.VMEM((batch, BLOCK, N_HEADS * BLOCK), F32)],
        compiler_params=_params(("parallel",)),
        name="stick_breaking",
    )(u3, u3, u3, nuo).reshape(batch * lt, BW)


def _constants(seq):
    idx = jnp.arange(BLOCK)
    tril = (idx[:, None] >= idx[None, :])
    nuo = -jnp.concatenate([tril, jnp.ones((BLOCK, BLOCK), bool)], axis=1).astype(BF16)
    c = {"tril": jnp.concatenate([tril, tril], axis=1).astype(BF16),
         "nuo": jnp.concatenate([nuo, nuo], axis=0)}
    head_e = jnp.arange(BW) // HEAD_DIM
    c["bd_gla"] = (head_e[:, None] == (jnp.arange(BLOCK) // GLA_DK)[None, :]).astype(F32)
    bd = head_e[:, None] == head_e[None, :]
    c["bd"] = bd.astype(F32)
    c["bdones"] = bd.astype(BF16)
    c["bdmean"] = (bd.astype(F32) / HEAD_DIM).astype(BF16)
    c["onestack"] = ((jnp.arange(2 * BLOCK) // BLOCK)[:, None]
                     == (jnp.arange(BLOCK) // HEAD_DIM)[None, :]).astype(BF16)
    lane = jnp.arange(BLOCK)[:, None]
    spread = jnp.concatenate([lane == I_LANE + head_e[None, :], lane == F_LANE + head_e[None, :]], axis=1)
    c["e3"] = jnp.concatenate([spread] * 3, axis=0).astype(BF16)
    positions = jnp.concatenate([jnp.arange(N_META, seq + N_META), jnp.arange(-N_FRONT_PAD, N_META)])
    half = HEAD_DIM // 2
    inv_freq = ROPE_BASE ** (-jnp.arange(half, dtype=F32) / half)
    ang = positions.astype(jnp.int32).astype(F32)[:, None] * inv_freq[None, :]
    cos, sin = jnp.cos(ang), jnp.sin(ang)
    c["cos"] = jnp.tile(jnp.concatenate([cos, cos], axis=1), (1, N_HEADS))
    c["sin"] = jnp.tile(jnp.concatenate([-sin, sin], axis=1), (1, N_HEADS))
    log_gamma = jnp.log1p(-jnp.exp2(-5.0 - jnp.arange(N_HEADS, dtype=F32)))
    pos = jnp.arange(BLOCK, dtype=F32)
    diff = jnp.maximum(pos[:, None] - pos[None, :], 0.0)
    dec = jnp.where(tril, jnp.exp(log_gamma[:, None, None] * diff), 0.0)
    c["ret_dec"] = jnp.moveaxis(dec, 0, 1).reshape(BLOCK, N_HEADS * BLOCK)
    c["ret_qdec"] = jnp.repeat(jnp.exp(log_gamma[:, None] * (pos + 1.0)).T, HEAD_DIM, axis=1)
    c["ret_kdec"] = jnp.repeat(jnp.exp(log_gamma[:, None] * (BLOCK - 1.0 - pos)).T, HEAD_DIM, axis=1)
    c["ret_gam"] = jnp.repeat(jnp.exp(log_gamma * BLOCK), HEAD_DIM)[None, :]
    return c


def _cols(w, name):
    a, b = _C[name]
    return w[..., a:b]


def _layout_weights(w_in):
    cols = lambda name: _cols(w_in, name).astype(BF16)
    pad = jnp.zeros(w_in.shape[:-1] + (BLOCK - GLA_LOW_RANK - 2 * N_HEADS,), BF16)
    return {
        "ml": jnp.concatenate([cols(n) for n in ("ml_qk", "ml_v", "ml_o")], axis=-1),
        "ret": cols("ret"),
        "small": jnp.concatenate([cols("gla_code"), cols("ml_i"), cols("ml_f"), pad], axis=-1),
        "merge": cols("merge"),
    }


def kernel(x, meta_tokens, norm_mix_pre, norm_mix_post, norm_ffn_pre, norm_ffn_post, w_in,
           gla_w_gate_up, gla_b_gate, gla_norm, ml_conv, ml_b_i, ml_b_f, ml_norm, ret_norm,
           w_branch, b_merge, w_out, ffn_w_gate, ffn_w_up, ffn_w_down):
    batch, seq, d = x.shape
    depth = w_in.shape[0]
    lt = seq + BLOCK
    pad = jnp.zeros((batch, N_FRONT_PAD, d), x.dtype)
    meta = jnp.broadcast_to(meta_tokens.astype(x.dtype)[None], (batch, N_META, d))
    h = jnp.concatenate([x, pad, meta], axis=1).reshape(batch * lt, d)

    c = _constants(seq)
    w = _layout_weights(w_in)
    wg_gla = jnp.pad(gla_w_gate_up, ((0, 0), (0, BLOCK - GLA_LOW_RANK), (0, 0))).astype(BF16)
    tail = BLOCK - GLA_LOW_RANK - 2 * N_HEADS
    b_gate = jnp.pad(jnp.concatenate([ml_b_i, ml_b_f], axis=-1), ((0, 0), (GLA_LOW_RANK, tail)))
    wb = w_branch.astype(BF16)
    wo = w_out.astype(BF16)
    wfg, wfu, wfd = ffn_w_gate.astype(BF16), ffn_w_up.astype(BF16), ffn_w_down.astype(BF16)
    tm = _largest_tile(lt, (640, 512, 384, 256, 128))

    for l in range(depth):
        hn, u_sb, u_gla, u_ml, u_ret = _row_call(
            functools.partial(_proj_kernel, tm=tm, seq=seq), "in_projection", batch, lt,
            [_full(h)], [c["cos"], c["sin"]],
            [norm_mix_pre[l][None], _w_in_group(w_in, l, U_SB, 0), _w_in_group(w_in, l, U_SB, 1),
             w["ml"][l], w["ret"][l], w["small"][l], wg_gla[l],
             gla_b_gate[l][None], b_gate[l][None]],
            [D_MODEL, U_SB, U_GLA, U_ML, U_RET], [BF16, BF16, F32, F32, F32])
        y_sb = _sb_call(u_sb, c["nuo"], batch, lt)
        o_gla, o_ml, o_ret = _recurrent_call(
            batch, lt, [u_gla, u_ml, u_ret],
            [ml_conv[l], c["tril"], c["e3"], c["bd_gla"], c["bd"], c["bdones"], c["onestack"],
             c["ret_dec"], c["ret_qdec"], c["ret_kdec"], c["ret_gam"]])
        (h,) = _row_call(
            _merge_kernel, "merge", batch, lt,
            [_full(hn), _full(h), _full(y_sb), _full(o_gla), _full(o_ml), _full(o_ret),
             (u_gla, BW, 2), (u_ml, BW, 3), (u_ret, BW, 3)], [],
            [w["merge"][l], b_merge[l].reshape(1, N_BRANCH * D_MODEL), wb[l], wo[l], norm_mix_post[l][None],
             gla_norm[l][None], ml_norm[l][None], ret_norm[l][None], c["bdmean"]],
            [D_MODEL], [F32])
        ffn_consts = [norm_ffn_pre[l][None], wfg[l], wfu[l], wfd[l], norm_ffn_post[l][None]]
        if l + 1 < depth:
            h = _ffn_call(h, batch, lt, ffn_consts).reshape(batch * lt, d)
        else:
            return _ffn_call(h, batch, lt, ffn_consts, final_seq=seq)
```

```python
import functools

import jax
import jax.numpy as jnp
from jax import lax
from jax.experimental import pallas as pl
from jax.experimental.pallas import tpu as pltpu

F32 = jnp.float32
BF16 = jnp.bfloat16

D_MODEL = 1024
BLOCK = 128
HEAD_DIM = 64
N_HEADS = 4
BW = N_HEADS * HEAD_DIM
N_BRANCH = 4
N_META = 16
N_FRONT_PAD = BLOCK - N_META
GLA_DK = 32
GLA_LOW_RANK = 16
GLA_GATE_NORMALIZER = 16.0
CONV_WIDTH = 4
D_FF = 2816
FF_CHUNK = 256
ROPE_BASE = 10000.0
NORM_EPS = 1e-6
SB_LOG_ZERO = -104.0
SB_STATIC_TILES = 3
V7X_VMEM_LIMIT = 56 * 1024 * 1024

CODE_LANE = 0
I_LANE = GLA_LOW_RANK
F_LANE = GLA_LOW_RANK + N_HEADS

_C = {}
_off = 0
for _name, _w in (("sb", 768), ("gla_q", 128), ("gla_k", 128), ("gla_v", 256), ("gla_r", 256), ("gla_code", 16),
                  ("ml_qk", 512), ("ml_v", 256), ("ml_i", 4), ("ml_f", 4), ("ml_o", 256), ("ret", 1024),
                  ("merge", 4096)):
    _C[_name] = (_off, _off + _w)
    _off += _w
IN_WIDTH = _off

U_SB, U_GLA, U_ML, U_RET = 768, 896, 1152, 1024


def _mm(a, b):
    return jnp.dot(a, b, preferred_element_type=F32)


def _mm_nt(a, b):
    return lax.dot_general(a, b, (((1,), (1,)), ((), ())), preferred_element_type=F32)


def _split(x):
    hi = x.astype(BF16)
    lo = (x - hi.astype(F32)).astype(BF16)
    return hi, lo


def _mm_split_r(c2, x):
    hi, lo = _split(x)
    return _mm(c2, jnp.concatenate([hi, lo], axis=0))


def _softplus(x):
    return jnp.maximum(x, 0.0) + jnp.log(1.0 + jnp.exp(-jnp.abs(x)))


def _log_sigmoid(x):
    return -_softplus(-x)


def _sigmoid(x):
    return 1.0 / (1.0 + jnp.exp(-x))


def _rmsnorm(x, g):
    return x * lax.rsqrt(jnp.mean(x * x, axis=-1, keepdims=True) + NORM_EPS) * g


def _head_rmsnorm(y, g, bdmean):
    ms = _mm((y * y).astype(BF16), bdmean)
    return y * lax.rsqrt(ms + NORM_EPS) * g


def _lanes_in(width, lo, hi):
    lane = lax.broadcasted_iota(jnp.int32, (1, width), 1)
    return (lane >= lo) & (lane < hi)


def _row_stack_heads(x, per_head, n=N_HEADS):
    keep = [_lanes_in(x.shape[1], per_head * h, per_head * (h + 1)).astype(x.dtype) for h in range(n)]
    return jnp.concatenate([x * m for m in keep], axis=0)


def _pair_scores(q_b, k_b):
    zs = []
    for p in range(N_HEADS // 2):
        lanes = slice(BLOCK * p, BLOCK * (p + 1))
        zs.append(_mm_nt(q_b[:, lanes], _row_stack_heads(k_b[:, lanes], HEAD_DIM, 2)))
    return jnp.concatenate(zs, axis=1)


def _pair_apply(s_b, v_b, extra=None):
    outs, extras = [], []
    for p in range(N_HEADS // 2):
        rhs = _row_stack_heads(v_b[:, BLOCK * p:BLOCK * (p + 1)], HEAD_DIM, 2)
        if extra is not None:
            rhs = jnp.concatenate([rhs, extra], axis=1)
        r = _mm(s_b[:, 2 * BLOCK * p:2 * BLOCK * (p + 1)], rhs)
        outs.append(r[:, :BLOCK])
        extras.append(r[:, BLOCK:])
    out = jnp.concatenate(outs, axis=1)
    return out if extra is None else (out, jnp.concatenate(extras, axis=1))


def _tile4(x):
    return jnp.concatenate([x] * N_HEADS, axis=1)


def _rotary(x, cos, sin_signed):
    first_half = _lanes_in(BLOCK, 0, 32) | _lanes_in(BLOCK, 64, 96)
    outs = []
    for t in range(2):
        lanes = slice(BLOCK * t, BLOCK * (t + 1))
        xt = x[:, lanes]
        swapped = jnp.where(first_half, pltpu.roll(xt, 96, 1), pltpu.roll(xt, 32, 1))
        outs.append(xt * cos[:, lanes] + swapped * sin_signed[:, lanes])
    return jnp.concatenate(outs, axis=1)


def _proj_kernel(h_ref, cos_ref, sin_ref, g_ref, wsb_ref, wgla_ref, wml_ref, wret_ref, wsmall_ref, wg_ref, bg_ref,
                 bgate_ref, hn_ref, usb_ref, ugla_ref, uml_ref, uret_ref, *, tm, seq):
    x = h_ref[...]
    pos = pl.program_id(1) * tm + lax.broadcasted_iota(jnp.int32, (tm, 1), 0)
    valid = ((pos < seq) | (pos >= seq + N_FRONT_PAD)).astype(F32)
    hn = (_rmsnorm(x, g_ref[...]) * valid).astype(BF16)
    hn_ref[...] = hn
    usb_ref[...] = _mm_nt(hn, wsb_ref[...].astype(BF16)).astype(BF16)
    small = _mm_nt(hn, wsmall_ref[...])
    ugla_ref[:, 0:768] = _mm_nt(hn, wgla_ref[...].astype(BF16))
    ugla_ref[:, 768:896] = (_log_sigmoid(_mm(small.astype(BF16), wg_ref[...]) + bg_ref[...])
                            * (1.0 / GLA_GATE_NORMALIZER))
    uml_ref[:, 0:1024] = _mm_nt(hn, wml_ref[...])
    gates = small + bgate_ref[...]
    uml_ref[:, 1024:1152] = jnp.where(_lanes_in(BLOCK, F_LANE, F_LANE + N_HEADS), _log_sigmoid(gates), gates)
    ret = _mm_nt(hn, wret_ref[...])
    cos = cos_ref[...]
    sin = sin_ref[...]
    uret_ref[:, 0:256] = _rotary(ret[:, 0:256], cos, sin)
    uret_ref[:, 256:512] = _rotary(ret[:, 256:512], cos, sin) * (HEAD_DIM ** -0.5)
    uret_ref[:, 512:1024] = ret[:, 512:1024]


def _sb_tile(q, k_ref, v_ref, j, n_blocks, bias, nuo2, cb):
    state = {"cb": cb, "out": None}
    for _ in _sb_tile_stages(q, k_ref, v_ref, j, n_blocks, bias, nuo2, state):
        pass
    return state["out"], state["cb"]


def _sb_tile_stages(q, k_ref, v_ref, j, n_blocks, bias, nuo2, state):
    start = pl.multiple_of(jnp.where(j <= 0, n_blocks - 1, j - 1) * BLOCK, BLOCK)
    kj = k_ref[pl.ds(start, BLOCK), :]
    zs = []
    for p in range(N_HEADS // 2):
        lanes = slice(BLOCK * p, BLOCK * (p + 1))
        k_rows = _row_stack_heads(kj[:, lanes], HEAD_DIM, 2)
        zs.append(_mm_nt(q[:, lanes], k_rows))
    yield
    z = jnp.concatenate(zs, axis=1) + bias
    hi, lo = _split(_softplus(z))
    rs = []
    for h in range(N_HEADS):
        lanes = slice(BLOCK * h, BLOCK * (h + 1))
        rs.append(_mm(jnp.concatenate([hi[:, lanes], lo[:, lanes]], axis=1), nuo2))
    yield
    cb = state["cb"]
    a = jnp.exp(z + jnp.concatenate([r[:, :BLOCK] for r in rs], axis=1) + cb).astype(BF16)
    state["cb"] = cb + jnp.concatenate([r[:, BLOCK:] for r in rs], axis=1)
    contrib = _mm(a, _row_stack_heads(v_ref[pl.ds(start, BLOCK), :], HEAD_DIM))
    yield
    acc_ref = state.get("acc_ref")
    if acc_ref is None:
        state["out"] = contrib if state["out"] is None else state["out"] + contrib
    elif state["out"] is None:
        acc_ref[...] = contrib
        state["out"] = acc_ref
    else:
        acc_ref[...] += contrib
    yield


def _sb_kernel(q_ref, k_ref, v_ref, nuo_ref, o_ref, acc_ref, cb_ref):
    i = pl.program_id(0)
    n_blocks = pl.num_programs(0)
    batch = q_ref.shape[0]
    qs = [q_ref[b] * jnp.asarray(HEAD_DIM ** -0.5, BF16) for b in range(batch)]
    row = lax.broadcasted_iota(jnp.int32, (BLOCK, BLOCK), 0)
    col = lax.broadcasted_iota(jnp.int32, (BLOCK, BLOCK), 1)
    nuo2 = nuo_ref[...]

    def bias_for(j, causal):
        ok = (j * BLOCK + col) >= N_FRONT_PAD
        if causal:
            ok = ok & (col < row)
        return jnp.concatenate([jnp.where(ok, 0.0, -jnp.inf)] * N_HEADS, axis=1)

    def all_max(x):
        return jnp.max(jnp.max(x, axis=1, keepdims=True), axis=0, keepdims=True)[0, 0]

    biases = [bias_for(i - t, t == 0) for t in range(SB_STATIC_TILES)]
    states = [{"cb": jnp.zeros((BLOCK, N_HEADS * BLOCK), F32), "out": None, "acc_ref": acc_ref.at[b]}
              for b in range(batch)]
    tiles = [_sb_tile_stages(qs[b], k_ref.at[b], v_ref.at[b], i - t, n_blocks, biases[t], nuo2, states[b])
             for t in range(SB_STATIC_TILES) for b in range(batch)]
    for _ in range(4):
        for tile in tiles:
            next(tile)

    def cond(c):
        j, mx = c
        return jnp.logical_and(j >= 0, mx > SB_LOG_ZERO)

    for b in range(batch):
        cb_ref[b] = states[b]["cb"]

        def body(c, b=b):
            j, _ = c
            contrib, cbn = _sb_tile(qs[b], k_ref.at[b], v_ref.at[b], j, n_blocks, bias_for(j, False), nuo2,
                                    cb_ref[b])
            acc_ref[b] += contrib
            cb_ref[b] = cbn
            return j - 1, all_max(cbn)

        lax.while_loop(cond, body, (i - SB_STATIC_TILES, all_max(states[b]["cb"])))
        o_ref[b] = acc_ref[b]


def _gla_chunk(u_ref, tril_ref, bd_ref, o_ref, st_ref):
    q = u_ref[:, 0:128] * (GLA_DK ** -0.5)
    k = u_ref[:, 128:256]
    v = u_ref[:, 256:512]
    log_a = u_ref[:, 768:896]
    cum = _mm_split_r(tril_ref[...], log_a)
    yield
    cum_last = cum[BLOCK - 1:BLOCK, :]
    qd = (q * jnp.exp(cum)).astype(BF16)
    kd = (k * jnp.exp(-cum)).astype(BF16)
    ke = (k * jnp.exp(cum_last - cum)).astype(BF16)
    st = st_ref[...]
    inter = _mm_nt(qd, st.astype(BF16))
    scores = _mm_nt(qd, _row_stack_heads(kd, GLA_DK))
    local = _mm(v.T.astype(BF16), ke)
    yield
    row = lax.broadcasted_iota(jnp.int32, (BLOCK, BLOCK), 0)
    col = lax.broadcasted_iota(jnp.int32, (BLOCK, BLOCK), 1)
    s = jnp.where(_tile4(row >= col), scores, 0.0)
    intra = _pair_apply(s.astype(BF16), v.astype(BF16))
    yield
    st_ref[...] = st * jnp.exp(cum_last) + local * bd_ref[...]
    o_ref[...] = intra + inter


def _ret_chunk(u_ref, dec_ref, qdec_ref, kdec_ref, gam_ref, bd_ref, o_ref, st_ref):
    qr = u_ref[:, 0:256]
    kr = u_ref[:, 256:512]
    v = u_ref[:, 512:768]
    scores = _pair_scores(qr.astype(BF16), kr.astype(BF16))
    st = st_ref[...]
    inter = _mm_nt((qr * qdec_ref[...]).astype(BF16), st.astype(BF16))
    local = _mm(v.T.astype(BF16), (kr * kdec_ref[...]).astype(BF16))
    yield
    s = scores * dec_ref[...]
    intra = _pair_apply(s.astype(BF16), v.astype(BF16))
    yield
    st_ref[...] = st * gam_ref[...] + local * bd_ref[...]
    o_ref[...] = intra + inter


def _spread_head(x_t, h):
    own = _lanes_in(BLOCK, 0, HEAD_DIM) if h % 2 == 0 else _lanes_in(BLOCK, HEAD_DIM, BLOCK)
    return jnp.where(own, x_t, pltpu.roll(x_t, HEAD_DIM, 1))


def _ml_chunk(u_ref, cw_ref, tril_ref, e3_ref, bd_ref, bdones_ref, onestack_ref, o_ref, xbuf_ref, ct_ref, n_ref,
              m_ref):
    gates = u_ref[:, 1024:1152]
    gates_cum = _mm_split_r(tril_ref[...], gates)
    xbuf_ref[8:8 + BLOCK, :] = u_ref[:, 0:512]
    conv = None
    for j in range(CONV_WIDTH):
        term = cw_ref[j:j + 1, :] * xbuf_ref[8 - (CONV_WIDTH - 1) + j:8 - (CONV_WIDTH - 1) + j + BLOCK, :]
        conv = term if conv is None else conv + term
    xbuf_ref[0:8, :] = xbuf_ref[BLOCK:BLOCK + 8, :]
    yield
    compact = jnp.where(_lanes_in(BLOCK, F_LANE, F_LANE + N_HEADS), gates_cum, gates)
    t1 = compact.astype(BF16)
    rest = compact - t1.astype(F32)
    t2 = rest.astype(BF16)
    t3 = (rest - t2.astype(F32)).astype(BF16)
    spread = _mm(jnp.concatenate([t1, t2, t3], axis=1), e3_ref[...])
    qk = conv * _sigmoid(conv)
    q = qk[:, 0:BW]
    k = qk[:, BW:2 * BW] * (HEAD_DIM ** -0.5)
    v = u_ref[:, 512:768]
    q_b = q.astype(BF16)
    k_b = k.astype(BF16)
    qk_all = _pair_scores(q_b, k_b)
    m_s = m_ref[0:1, :]
    n_s = n_ref[0:1, :]
    ct = ct_ref[...]
    q_ct = _mm_nt(q_b, ct.astype(BF16))
    q_n = _mm((q * n_s).astype(BF16), bdones_ref[...])
    yield
    log_i = spread[:, 0:BW]
    cum = spread[:, BW:2 * BW]

    cum_last = cum[BLOCK - 1:BLOCK, :]
    w_end = cum_last - cum + log_i
    a_end = jnp.max(w_end, axis=0, keepdims=True)
    kp = k * jnp.exp(w_end - a_end)
    v_b = v.astype(BF16)
    c_loc = _mm(v.T.astype(BF16), kp.astype(BF16)) * bd_ref[...]
    n_loc = jnp.sum(kp, axis=0, keepdims=True)

    inter_log = cum + m_s
    d_t = (log_i - cum).T
    yield
    row = lax.broadcasted_iota(jnp.int32, (BLOCK, BLOCK), 0)
    col = lax.broadcasted_iota(jnp.int32, (BLOCK, BLOCK), 1)
    s_parts, m_parts = [], []
    for h in range(N_HEADS):
        if h == N_HEADS // 2:
            yield
        t = h // 2
        cq = _spread_head(cum[:, BLOCK * t:BLOCK * (t + 1)], h)
        ilq = _spread_head(inter_log[:, BLOCK * t:BLOCK * (t + 1)], h)
        dk = jnp.broadcast_to(d_t[HEAD_DIM * h:HEAD_DIM * h + 1, :], (BLOCK, BLOCK))
        intra_log = jnp.where(row >= col, cq + dk, -jnp.inf)
        m_h = jnp.maximum(ilq, jnp.max(intra_log, axis=1, keepdims=True))
        s_parts.append(qk_all[:, BLOCK * h:BLOCK * (h + 1)] * jnp.exp(intra_log - m_h))
        m_parts.append(m_h)
    first = _lanes_in(BLOCK, 0, HEAD_DIM)
    m_t = jnp.concatenate([jnp.where(first, m_parts[0], m_parts[1]),
                           jnp.where(first, m_parts[2], m_parts[3])], axis=1)
    w_inter = jnp.exp(inter_log - m_t)
    s_b = jnp.concatenate(s_parts, axis=1).astype(BF16)
    num_intra, den_intra = _pair_apply(s_b, v_b, extra=onestack_ref[...])
    yield
    m_new = jnp.maximum(cum_last + m_s, a_end)
    s_prev = jnp.exp(cum_last + m_s - m_new)
    s_loc = jnp.exp(a_end - m_new)
    ct_ref[...] = s_prev * ct + s_loc * c_loc
    n_ref[...] = jnp.broadcast_to(s_prev * n_s + s_loc * n_loc, n_ref.shape)
    m_ref[...] = jnp.broadcast_to(m_new, m_ref.shape)
    num = num_intra + w_inter * q_ct
    den = den_intra + w_inter * q_n
    o_ref[...] = num / jnp.maximum(jnp.abs(den), jnp.exp(-m_t))


def _recurrent_kernel(ugla_ref, uml_ref, uret_ref, cw_ref, tril_ref, e3_ref, bdgla_ref, bd_ref, bdones_ref,
                      onestack_ref, dec_ref, qdec_ref, kdec_ref, gam_ref,
                      ogla_ref, oml_ref, oret_ref,
                      stgla_ref, xbuf_ref, ct_ref, n_ref, m_ref, stret_ref):
    @pl.when(pl.program_id(0) == 0)
    def _():
        for ref in (stgla_ref, xbuf_ref, ct_ref, n_ref, m_ref, stret_ref):
            ref[...] = jnp.zeros_like(ref)

    chains = []
    for b in range(ugla_ref.shape[0]):
        chains += [
            _ml_chunk(uml_ref.at[b], cw_ref, tril_ref, e3_ref, bd_ref, bdones_ref, onestack_ref,
                      oml_ref.at[b], xbuf_ref.at[b], ct_ref.at[b], n_ref.at[b], m_ref.at[b]),
            _gla_chunk(ugla_ref.at[b], tril_ref, bdgla_ref, ogla_ref.at[b], stgla_ref.at[b]),
            _ret_chunk(uret_ref.at[b], dec_ref, qdec_ref, kdec_ref, gam_ref, bd_ref, oret_ref.at[b],
                       stret_ref.at[b])]
    while chains:
        chains = [chain for chain in chains if next(chain, True) is None]


def _merge_kernel(hn_ref, h_ref, ysb_ref, ogla_ref, oml_ref, oret_ref, r_ref, opre_ref, gret_ref,
                  wm_ref, bm_ref, wb_ref, wo_ref, g_ref, ngla_ref, nml_ref, nret_ref, bdmean_ref, o_ref):
    hn = hn_ref[...]
    bdmean = bdmean_ref[...]
    r = r_ref[...]
    g = gret_ref[...]
    branches = (ysb_ref[...],
                _head_rmsnorm(ogla_ref[...], ngla_ref[...], bdmean) * (r * _sigmoid(r)),
                _sigmoid(opre_ref[...]) * _head_rmsnorm(oml_ref[...], nml_ref[...], bdmean),
                (g * _sigmoid(g)) * _head_rmsnorm(oret_ref[...], nret_ref[...], bdmean))
    merged = None
    for n, y in enumerate(branches):
        cols = slice(D_MODEL * n, D_MODEL * (n + 1))
        gate = _sigmoid(_mm_nt(hn, wm_ref[cols, :]) + bm_ref[:, cols])
        term = gate * _mm(y.astype(BF16), wb_ref[n])
        merged = term if merged is None else merged + term
    mix_out = _mm(merged.astype(BF16), wo_ref[...])
    o_ref[...] = h_ref[...] + _rmsnorm(mix_out, g_ref[...])


def _ffn_kernel(h_ref, gpre_ref, wg_ref, wu_ref, wd_ref, gpost_ref, o_ref, acc_ref):
    x = h_ref[...]
    f = _rmsnorm(x, gpre_ref[...]).astype(BF16)
    for c in range(D_FF // FF_CHUNK):
        cols = slice(FF_CHUNK * c, FF_CHUNK * (c + 1))
        a = _mm(f, wg_ref[:, cols])
        act = ((a * _sigmoid(a)) * _mm(f, wu_ref[:, cols])).astype(BF16)
        part = _mm(act, wd_ref[cols, :])
        if c == 0:
            acc_ref[...] = part
        else:
            acc_ref[...] += part
    o_ref[...] = x + _rmsnorm(acc_ref[...], gpost_ref[...])


def _largest_tile(n, candidates):
    for tm in candidates:
        if n % tm == 0:
            return tm
    raise ValueError(f"{n} rows are not a multiple of {BLOCK}")


def _const_spec(shape, grid_rank):
    zeros = (0,) * len(shape)
    return pl.BlockSpec(shape, (lambda b, j: zeros) if grid_rank == 2 else (lambda n: zeros))


def _params(sem):
    return pltpu.CompilerParams(dimension_semantics=sem, vmem_limit_bytes=V7X_VMEM_LIMIT)


def _row_call(kernel, name, batch, lt, row_inputs, tile_inputs, const_inputs, out_widths, out_dtypes):
    tm = _largest_tile(lt, (640, 512, 384, 256, 128))
    nb = lt // tm
    rows = batch * lt
    row_spec = lambda w, cb: pl.BlockSpec((tm, w), lambda b, j: (b * nb + j, cb))
    return pl.pallas_call(
        kernel,
        grid=(batch, nb),
        in_specs=([row_spec(w, cb) for _, w, cb in row_inputs]
                  + [pl.BlockSpec((tm, a.shape[1]), lambda b, j: (j, 0)) for a in tile_inputs]
                  + [a[1] if isinstance(a, tuple) else _const_spec(a.shape, 2) for a in const_inputs]),
        out_specs=[row_spec(w, 0) for w in out_widths],
        out_shape=[jax.ShapeDtypeStruct((rows, w), dt) for w, dt in zip(out_widths, out_dtypes)],
        compiler_params=_params(("parallel", "parallel")),
        name=name,
    )(*[a for a, _, _ in row_inputs], *tile_inputs, *[a[0] if isinstance(a, tuple) else a for a in const_inputs])


def _w_in_group(w_in_t, layer, width, index):
    return (w_in_t, pl.BlockSpec((None, width, D_MODEL), lambda b, j: (layer, index, 0)))


def _full(a):
    return (a, a.shape[1], 0)


def _ffn_call(h, batch, lt, consts, final_seq=None):
    if final_seq is None:
        tm, rows_per_batch, out_rows = _largest_tile(lt, (640, 512, 384, 256, 128)), lt, lt
    else:
        tm, rows_per_batch, out_rows = _largest_tile(final_seq, (512, 256, 128)), final_seq, final_seq
    spec = pl.BlockSpec((None, tm, D_MODEL), lambda b, j: (b, j, 0))
    return pl.pallas_call(
        _ffn_kernel,
        grid=(batch, rows_per_batch // tm),
        in_specs=[spec] + [_const_spec(a.shape, 2) for a in consts],
        out_specs=spec,
        out_shape=jax.ShapeDtypeStruct((batch, out_rows, D_MODEL), F32),
        scratch_shapes=[pltpu.VMEM((tm, D_MODEL), F32)],
        compiler_params=_params(("parallel", "parallel")),
        name="swiglu",
    )(h.reshape(batch, lt, D_MODEL), *consts)


def _logical_to_memory_block(n, nc):
    return jnp.where(n == 0, nc - 1, n - 1)


def _recurrent_call(batch, lt, u_list, const_inputs):
    nc = lt // BLOCK
    u3 = [u.reshape(batch, lt, u.shape[1]) for u in u_list]
    chunk_map = lambda n: (0, _logical_to_memory_block(n, nc), 0)
    o_spec = pl.BlockSpec((batch, BLOCK, BW), chunk_map)
    outs = pl.pallas_call(
        _recurrent_kernel,
        grid=(nc,),
        in_specs=([pl.BlockSpec((batch, BLOCK, u.shape[2]), chunk_map) for u in u3]
                  + [_const_spec(a.shape, 1) for a in const_inputs]),
        out_specs=[o_spec] * 3,
        out_shape=[jax.ShapeDtypeStruct((batch, lt, BW), F32)] * 3,
        scratch_shapes=[pltpu.VMEM((batch, BW, BLOCK), F32),
                        pltpu.VMEM((batch, BLOCK + 8, 2 * BW), F32),
                        pltpu.VMEM((batch, BW, BW), F32),
                        pltpu.VMEM((batch, 8, BW), F32),
                        pltpu.VMEM((batch, 8, BW), F32),
                        pltpu.VMEM((batch, BW, BW), F32)],
        compiler_params=_params(("arbitrary",)),
        name="recurrent_mixers",
    )(*u3, *const_inputs)
    return [o.reshape(batch * lt, BW) for o in outs]


def _sb_call(u_sb, nuo, batch, lt):
    nc = lt // BLOCK
    u3 = u_sb.reshape(batch, lt, U_SB)
    q_spec = pl.BlockSpec((batch, BLOCK, BW), lambda i: (0, _logical_to_memory_block(i, nc), 0))
    return pl.pallas_call(
        _sb_kernel,
        grid=(nc,),
        in_specs=[q_spec,
                  pl.BlockSpec((batch, lt, BW), lambda i: (0, 0, 1)),
                  pl.BlockSpec((batch, lt, BW), lambda i: (0, 0, 2)),
                  _const_spec(nuo.shape, 1)],
        out_specs=q_spec,
        out_shape=jax.ShapeDtypeStruct((batch, lt, BW), F32),
        scratch_shapes=[pltpu.VMEM((batch, BLOCK, BW), F32), pltpu.VMEM((batch, BLOCK, N_HEADS * BLOCK), F32)],
        compiler_params=_params(("parallel",)),
        name="stick_breaking",
    )(u3, u3, u3, nuo).reshape(batch * lt, BW)


def _constants(seq):
    idx = jnp.arange(BLOCK)
    tril = (idx[:, None] >= idx[None, :])
    nuo = -jnp.concatenate([tril, jnp.ones((BLOCK, BLOCK), bool)], axis=1).astype(BF16)
    c = {"tril": jnp.concatenate([tril, tril], axis=1).astype(BF16),
         "nuo": jnp.concatenate([nuo, nuo], axis=0)}
    head_e = jnp.arange(BW) // HEAD_DIM
    c["bd_gla"] = (head_e[:, None] == (jnp.arange(BLOCK) // GLA_DK)[None, :]).astype(F32)
    bd = head_e[:, None] == head_e[None, :]
    c["bd"] = bd.astype(F32)
    c["bdones"] = bd.astype(BF16)
    c["bdmean"] = (bd.astype(F32) / HEAD_DIM).astype(BF16)
    c["onestack"] = ((jnp.arange(2 * BLOCK) // BLOCK)[:, None]
                     == (jnp.arange(BLOCK) // HEAD_DIM)[None, :]).astype(BF16)
    lane = jnp.arange(BLOCK)[:, None]
    spread = jnp.concatenate([lane == I_LANE + head_e[None, :], lane == F_LANE + head_e[None, :]], axis=1)
    c["e3"] = jnp.concatenate([spread] * 3, axis=0).astype(BF16)
    positions = jnp.concatenate([jnp.arange(N_META, seq + N_META), jnp.arange(-N_FRONT_PAD, N_META)])
    half = HEAD_DIM // 2
    inv_freq = ROPE_BASE ** (-jnp.arange(half, dtype=F32) / half)
    ang = positions.astype(jnp.int32).astype(F32)[:, None] * inv_freq[None, :]
    cos, sin = jnp.cos(ang), jnp.sin(ang)
    c["cos"] = jnp.tile(jnp.concatenate([cos, cos], axis=1), (1, N_HEADS))
    c["sin"] = jnp.tile(jnp.concatenate([-sin, sin], axis=1), (1, N_HEADS))
    log_gamma = jnp.log1p(-jnp.exp2(-5.0 - jnp.arange(N_HEADS, dtype=F32)))
    pos = jnp.arange(BLOCK, dtype=F32)
    diff = jnp.maximum(pos[:, None] - pos[None, :], 0.0)
    dec = jnp.where(tril, jnp.exp(log_gamma[:, None, None] * diff), 0.0)
    c["ret_dec"] = jnp.moveaxis(dec, 0, 1).reshape(BLOCK, N_HEADS * BLOCK)
    c["ret_qdec"] = jnp.repeat(jnp.exp(log_gamma[:, None] * (pos + 1.0)).T, HEAD_DIM, axis=1)
    c["ret_kdec"] = jnp.repeat(jnp.exp(log_gamma[:, None] * (BLOCK - 1.0 - pos)).T, HEAD_DIM, axis=1)
    c["ret_gam"] = jnp.repeat(jnp.exp(log_gamma * BLOCK), HEAD_DIM)[None, :]
    return c


def _layout_weights(w_in_t):
    rows = lambda name: w_in_t[:, _C[name][0]:_C[name][1], :].astype(BF16)
    pad = jnp.zeros((w_in_t.shape[0], BLOCK - GLA_LOW_RANK - 2 * N_HEADS, D_MODEL), BF16)
    return {
        "ml": jnp.concatenate([rows(n) for n in ("ml_qk", "ml_v", "ml_o")], axis=1),
        "ret": rows("ret"),
        "small": jnp.concatenate([rows("gla_code"), rows("ml_i"), rows("ml_f"), pad], axis=1),
        "merge": rows("merge"),
    }


def kernel(x, meta_tokens, norm_mix_pre, norm_mix_post, norm_ffn_pre, norm_ffn_post, w_in,
           gla_w_gate_up, gla_b_gate, gla_norm, ml_conv, ml_b_i, ml_b_f, ml_norm, ret_norm,
           w_branch, b_merge, w_out, ffn_w_gate, ffn_w_up, ffn_w_down):
    batch, seq, d = x.shape
    depth = w_in.shape[0]
    lt = seq + BLOCK
    pad = jnp.zeros((batch, N_FRONT_PAD, d), x.dtype)
    meta = jnp.broadcast_to(meta_tokens.astype(x.dtype)[None], (batch, N_META, d))
    h = jnp.concatenate([x, pad, meta], axis=1).reshape(batch * lt, d)

    c = _constants(seq)
    w_in_t = jnp.swapaxes(w_in, 1, 2)
    w = _layout_weights(w_in_t)
    wg_gla = jnp.pad(gla_w_gate_up, ((0, 0), (0, BLOCK - GLA_LOW_RANK), (0, 0))).astype(BF16)
    tail = BLOCK - GLA_LOW_RANK - 2 * N_HEADS
    b_gate = jnp.pad(jnp.concatenate([ml_b_i, ml_b_f], axis=-1), ((0, 0), (GLA_LOW_RANK, tail)))
    wb = w_branch.astype(BF16)
    wo = w_out.astype(BF16)
    wfg, wfu, wfd = ffn_w_gate.astype(BF16), ffn_w_up.astype(BF16), ffn_w_down.astype(BF16)
    tm = _largest_tile(lt, (640, 512, 384, 256, 128))

    for l in range(depth):
        hn, u_sb, u_gla, u_ml, u_ret = _row_call(
            functools.partial(_proj_kernel, tm=tm, seq=seq), "in_projection", batch, lt,
            [_full(h)], [c["cos"], c["sin"]],
            [norm_mix_pre[l][None], _w_in_group(w_in_t, l, U_SB, 0), _w_in_group(w_in_t, l, U_SB, 1),
             w["ml"][l], w["ret"][l], w["small"][l], wg_gla[l],
             gla_b_gate[l][None], b_gate[l][None]],
            [D_MODEL, U_SB, U_GLA, U_ML, U_RET], [BF16, BF16, F32, F32, F32])
        y_sb = _sb_call(u_sb, c["nuo"], batch, lt)
        o_gla, o_ml, o_ret = _recurrent_call(
            batch, lt, [u_gla, u_ml, u_ret],
            [ml_conv[l], c["tril"], c["e3"], c["bd_gla"], c["bd"], c["bdones"], c["onestack"],
             c["ret_dec"], c["ret_qdec"], c["ret_kdec"], c["ret_gam"]])
        (h,) = _row_call(
            _merge_kernel, "merge", batch, lt,
            [_full(hn), _full(h), _full(y_sb), _full(o_gla), _full(o_ml), _full(o_ret),
             (u_gla, BW, 2), (u_ml, BW, 3), (u_ret, BW, 3)], [],
            [w["merge"][l], b_merge[l].reshape(1, N_BRANCH * D_MODEL), wb[l], wo[l], norm_mix_post[l][None],
             gla_norm[l][None], ml_norm[l][None], ret_norm[l][None], c["bdmean"]],
            [D_MODEL], [F32])
        ffn_consts = [norm_ffn_pre[l][None], wfg[l], wfu[l], wfd[l], norm_ffn_post[l][None]]
        if l + 1 < depth:
            h = _ffn_call(h, batch, lt, ffn_consts).reshape(batch * lt, d)
        else:
            return _ffn_call(h, batch, lt, ffn_consts, final_seq=seq)
```

```python
import functools

import jax
import jax.numpy as jnp
from jax import lax
from jax.experimental import pallas as pl
from jax.experimental.pallas import tpu as pltpu

F32 = jnp.float32
BF16 = jnp.bfloat16

D_MODEL = 1024
BLOCK = 128
HEAD_DIM = 64
N_HEADS = 4
BW = N_HEADS * HEAD_DIM
N_BRANCH = 4
N_META = 16
N_FRONT_PAD = BLOCK - N_META
GLA_DK = 32
GLA_LOW_RANK = 16
GLA_GATE_NORMALIZER = 16.0
CONV_WIDTH = 4
D_FF = 2816
FF_CHUNK = 256
ROPE_BASE = 10000.0
NORM_EPS = 1e-6
SB_LOG_ZERO = -104.0
SB_STATIC_TILES = 3
V7X_VMEM_LIMIT = 56 * 1024 * 1024

CODE_LANE = 0
I_LANE = GLA_LOW_RANK
F_LANE = GLA_LOW_RANK + N_HEADS

_C = {}
_off = 0
for _name, _w in (("sb", 768), ("gla_q", 128), ("gla_k", 128), ("gla_v", 256), ("gla_r", 256), ("gla_code", 16),
                  ("ml_qk", 512), ("ml_v", 256), ("ml_i", 4), ("ml_f", 4), ("ml_o", 256), ("ret", 1024),
                  ("merge", 4096)):
    _C[_name] = (_off, _off + _w)
    _off += _w
IN_WIDTH = _off

U_SB, U_GLA, U_ML, U_RET = 768, 896, 1152, 1024


def _mm(a, b):
    return jnp.dot(a, b, preferred_element_type=F32)


def _mm_nt(a, b):
    return lax.dot_general(a, b, (((1,), (1,)), ((), ())), preferred_element_type=F32)


def _split(x):
    hi = x.astype(BF16)
    lo = (x - hi.astype(F32)).astype(BF16)
    return hi, lo


def _mm_split_r(c2, x):
    hi, lo = _split(x)
    return _mm(c2, jnp.concatenate([hi, lo], axis=0))


def _softplus(x):
    return jnp.maximum(x, 0.0) + jnp.log(1.0 + jnp.exp(-jnp.abs(x)))


def _log_sigmoid(x):
    return -_softplus(-x)


def _sigmoid(x):
    return 1.0 / (1.0 + jnp.exp(-x))


def _rmsnorm(x, g):
    return x * lax.rsqrt(jnp.mean(x * x, axis=-1, keepdims=True) + NORM_EPS) * g


def _head_rmsnorm(y, g, bdmean):
    ms = _mm((y * y).astype(BF16), bdmean)
    return y * lax.rsqrt(ms + NORM_EPS) * g


def _lanes_in(width, lo, hi):
    lane = lax.broadcasted_iota(jnp.int32, (1, width), 1)
    return (lane >= lo) & (lane < hi)


def _row_stack_heads(x, per_head, n=N_HEADS):
    keep = [_lanes_in(x.shape[1], per_head * h, per_head * (h + 1)).astype(x.dtype) for h in range(n)]
    return jnp.concatenate([x * m for m in keep], axis=0)


def _pair_scores(q_b, k_b):
    zs = []
    for p in range(N_HEADS // 2):
        lanes = slice(BLOCK * p, BLOCK * (p + 1))
        zs.append(_mm_nt(q_b[:, lanes], _row_stack_heads(k_b[:, lanes], HEAD_DIM, 2)))
    return jnp.concatenate(zs, axis=1)


def _pair_apply(s_b, v_b, extra=None):
    outs, extras = [], []
    for p in range(N_HEADS // 2):
        rhs = _row_stack_heads(v_b[:, BLOCK * p:BLOCK * (p + 1)], HEAD_DIM, 2)
        if extra is not None:
            rhs = jnp.concatenate([rhs, extra], axis=1)
        r = _mm(s_b[:, 2 * BLOCK * p:2 * BLOCK * (p + 1)], rhs)
        outs.append(r[:, :BLOCK])
        extras.append(r[:, BLOCK:])
    out = jnp.concatenate(outs, axis=1)
    return out if extra is None else (out, jnp.concatenate(extras, axis=1))


def _tile4(x):
    return jnp.concatenate([x] * N_HEADS, axis=1)


def _rotary(x, cos, sin_signed):
    first_half = _lanes_in(BLOCK, 0, 32) | _lanes_in(BLOCK, 64, 96)
    outs = []
    for t in range(2):
        lanes = slice(BLOCK * t, BLOCK * (t + 1))
        xt = x[:, lanes]
        swapped = jnp.where(first_half, pltpu.roll(xt, 96, 1), pltpu.roll(xt, 32, 1))
        outs.append(xt * cos[:, lanes] + swapped * sin_signed[:, lanes])
    return jnp.concatenate(outs, axis=1)


def _proj_kernel(h_ref, cos_ref, sin_ref, g_ref, wsb_ref, wgla_ref, wml_ref, wret_ref, wsmall_ref, wg_ref, bg_ref,
                 bgate_ref, hn_ref, usb_ref, ugla_ref, uml_ref, uret_ref, *, tm, seq):
    x = h_ref[...]
    pos = pl.program_id(1) * tm + lax.broadcasted_iota(jnp.int32, (tm, 1), 0)
    valid = ((pos < seq) | (pos >= seq + N_FRONT_PAD)).astype(F32)
    hn = (_rmsnorm(x, g_ref[...]) * valid).astype(BF16)
    hn_ref[...] = hn
    small = _mm_nt(hn, wsmall_ref[...])
    ret = _mm_nt(hn, wret_ref[...])
    log_a_pre = _mm(small.astype(BF16), wg_ref[...])
    usb_ref[...] = _mm_nt(hn, wsb_ref[...].astype(BF16)).astype(BF16)
    cos = cos_ref[...]
    sin = sin_ref[...]
    uret_ref[:, 0:256] = _rotary(ret[:, 0:256], cos, sin)
    uret_ref[:, 256:512] = _rotary(ret[:, 256:512], cos, sin) * (HEAD_DIM ** -0.5)
    uret_ref[:, 512:1024] = ret[:, 512:1024]
    ugla_ref[:, 768:896] = _log_sigmoid(log_a_pre + bg_ref[...]) * (1.0 / GLA_GATE_NORMALIZER)
    gates = small + bgate_ref[...]
    uml_ref[:, 1024:1152] = jnp.where(_lanes_in(BLOCK, F_LANE, F_LANE + N_HEADS), _log_sigmoid(gates), gates)
    ugla_ref[:, 0:768] = _mm_nt(hn, wgla_ref[...].astype(BF16))
    uml_ref[:, 0:1024] = _mm_nt(hn, wml_ref[...])


def _sb_tile(q, k_ref, v_ref, j, n_blocks, bias, nuo2, cb):
    state = {"cb": cb, "out": None}
    for _ in _sb_tile_stages(q, k_ref, v_ref, j, n_blocks, bias, nuo2, state):
        pass
    return state["out"], state["cb"]


def _sb_tile_stages(q, k_ref, v_ref, j, n_blocks, bias, nuo2, state):
    start = pl.multiple_of(jnp.where(j <= 0, n_blocks - 1, j - 1) * BLOCK, BLOCK)
    kj = k_ref[pl.ds(start, BLOCK), :]
    zs = []
    for p in range(N_HEADS // 2):
        lanes = slice(BLOCK * p, BLOCK * (p + 1))
        k_rows = _row_stack_heads(kj[:, lanes], HEAD_DIM, 2)
        zs.append(_mm_nt(q[:, lanes], k_rows))
    yield
    z = jnp.concatenate(zs, axis=1) + bias
    hi, lo = _split(_softplus(z))
    rs = []
    for h in range(N_HEADS):
        lanes = slice(BLOCK * h, BLOCK * (h + 1))
        rs.append(_mm(jnp.concatenate([hi[:, lanes], lo[:, lanes]], axis=1), nuo2))
    yield
    cb = state["cb"]
    a = jnp.exp(z + jnp.concatenate([r[:, :BLOCK] for r in rs], axis=1) + cb).astype(BF16)
    state["cb"] = cb + jnp.concatenate([r[:, BLOCK:] for r in rs], axis=1)
    contrib = _mm(a, _row_stack_heads(v_ref[pl.ds(start, BLOCK), :], HEAD_DIM))
    yield
    acc_ref = state.get("acc_ref")
    if acc_ref is None:
        state["out"] = contrib if state["out"] is None else state["out"] + contrib
    elif state["out"] is None:
        acc_ref[...] = contrib
        state["out"] = acc_ref
    else:
        acc_ref[...] += contrib
    yield


def _sb_kernel(q_ref, k_ref, v_ref, nuo_ref, o_ref, acc_ref, cb_ref):
    i = pl.program_id(0)
    n_blocks = pl.num_programs(0)
    batch = q_ref.shape[0]
    qs = [q_ref[b] * jnp.asarray(HEAD_DIM ** -0.5, BF16) for b in range(batch)]
    row = lax.broadcasted_iota(jnp.int32, (BLOCK, BLOCK), 0)
    col = lax.broadcasted_iota(jnp.int32, (BLOCK, BLOCK), 1)
    nuo2 = nuo_ref[...]

    def bias_for(j, causal):
        ok = (j * BLOCK + col) >= N_FRONT_PAD
        if causal:
            ok = ok & (col < row)
        return jnp.concatenate([jnp.where(ok, 0.0, -jnp.inf)] * N_HEADS, axis=1)

    def all_max(x):
        return jnp.max(jnp.max(x, axis=1, keepdims=True), axis=0, keepdims=True)[0, 0]

    biases = [bias_for(i - t, t == 0) for t in range(SB_STATIC_TILES)]
    states = [{"cb": jnp.zeros((BLOCK, N_HEADS * BLOCK), F32), "out": None, "acc_ref": acc_ref.at[b]}
              for b in range(batch)]
    tiles = [_sb_tile_stages(qs[b], k_ref.at[b], v_ref.at[b], i - t, n_blocks, biases[t], nuo2, states[b])
             for t in range(SB_STATIC_TILES) for b in range(batch)]
    for _ in range(4):
        for tile in tiles:
            next(tile)

    def cond(c):
        j, mx = c
        return jnp.logical_and(j >= 0, mx > SB_LOG_ZERO)

    for b in range(batch):
        cb_ref[b] = states[b]["cb"]

        def body(c, b=b):
            j, _ = c
            contrib, cbn = _sb_tile(qs[b], k_ref.at[b], v_ref.at[b], j, n_blocks, bias_for(j, False), nuo2,
                                    cb_ref[b])
            acc_ref[b] += contrib
            cb_ref[b] = cbn
            return j - 1, all_max(cbn)

        lax.while_loop(cond, body, (i - SB_STATIC_TILES, all_max(states[b]["cb"])))
        o_ref[b] = acc_ref[b]


def _gla_chunk(u_ref, tril_ref, bd_ref, o_ref, st_ref):
    q = u_ref[:, 0:128] * (GLA_DK ** -0.5)
    k = u_ref[:, 128:256]
    v = u_ref[:, 256:512]
    log_a = u_ref[:, 768:896]
    cum = _mm_split_r(tril_ref[...], log_a)
    yield
    cum_last = cum[BLOCK - 1:BLOCK, :]
    qd = (q * jnp.exp(cum)).astype(BF16)
    kd = (k * jnp.exp(-cum)).astype(BF16)
    ke = (k * jnp.exp(cum_last - cum)).astype(BF16)
    st = st_ref[...]
    inter = _mm_nt(qd, st.astype(BF16))
    scores = _mm_nt(qd, _row_stack_heads(kd, GLA_DK))
    local = _mm(v.T.astype(BF16), ke)
    yield
    row = lax.broadcasted_iota(jnp.int32, (BLOCK, BLOCK), 0)
    col = lax.broadcasted_iota(jnp.int32, (BLOCK, BLOCK), 1)
    s = jnp.where(_tile4(row >= col), scores, 0.0)
    intra = _pair_apply(s.astype(BF16), v.astype(BF16))
    yield
    st_ref[...] = st * jnp.exp(cum_last) + local * bd_ref[...]
    o_ref[...] = intra + inter


def _ret_chunk(u_ref, dec_ref, qdec_ref, kdec_ref, gam_ref, bd_ref, o_ref, st_ref):
    qr = u_ref[:, 0:256]
    kr = u_ref[:, 256:512]
    v = u_ref[:, 512:768]
    scores = _pair_scores(qr.astype(BF16), kr.astype(BF16))
    st = st_ref[...]
    inter = _mm_nt((qr * qdec_ref[...]).astype(BF16), st.astype(BF16))
    local = _mm(v.T.astype(BF16), (kr * kdec_ref[...]).astype(BF16))
    yield
    s = scores * dec_ref[...]
    intra = _pair_apply(s.astype(BF16), v.astype(BF16))
    yield
    st_ref[...] = st * gam_ref[...] + local * bd_ref[...]
    o_ref[...] = intra + inter


def _spread_head(x_t, h):
    own = _lanes_in(BLOCK, 0, HEAD_DIM) if h % 2 == 0 else _lanes_in(BLOCK, HEAD_DIM, BLOCK)
    return jnp.where(own, x_t, pltpu.roll(x_t, HEAD_DIM, 1))


def _ml_chunk(u_ref, cw_ref, tril_ref, e3_ref, bd_ref, bdones_ref, onestack_ref, o_ref, xbuf_ref, ct_ref, n_ref,
              m_ref):
    gates = u_ref[:, 1024:1152]
    gates_cum = _mm_split_r(tril_ref[...], gates)
    xbuf_ref[8:8 + BLOCK, :] = u_ref[:, 0:512]
    conv = None
    for j in range(CONV_WIDTH):
        term = cw_ref[j:j + 1, :] * xbuf_ref[8 - (CONV_WIDTH - 1) + j:8 - (CONV_WIDTH - 1) + j + BLOCK, :]
        conv = term if conv is None else conv + term
    xbuf_ref[0:8, :] = xbuf_ref[BLOCK:BLOCK + 8, :]
    yield
    compact = jnp.where(_lanes_in(BLOCK, F_LANE, F_LANE + N_HEADS), gates_cum, gates)
    t1 = compact.astype(BF16)
    rest = compact - t1.astype(F32)
    t2 = rest.astype(BF16)
    t3 = (rest - t2.astype(F32)).astype(BF16)
    spread = _mm(jnp.concatenate([t1, t2, t3], axis=1), e3_ref[...])
    qk = conv * _sigmoid(conv)
    q = qk[:, 0:BW]
    k = qk[:, BW:2 * BW] * (HEAD_DIM ** -0.5)
    v = u_ref[:, 512:768]
    q_b = q.astype(BF16)
    k_b = k.astype(BF16)
    qk_all = _pair_scores(q_b, k_b)
    m_s = m_ref[0:1, :]
    n_s = n_ref[0:1, :]
    ct = ct_ref[...]
    q_ct = _mm_nt(q_b, ct.astype(BF16))
    q_n = _mm((q * n_s).astype(BF16), bdones_ref[...])
    yield
    log_i = spread[:, 0:BW]
    cum = spread[:, BW:2 * BW]

    cum_last = cum[BLOCK - 1:BLOCK, :]
    w_end = cum_last - cum + log_i
    a_end = jnp.max(w_end, axis=0, keepdims=True)
    kp = k * jnp.exp(w_end - a_end)
    v_b = v.astype(BF16)
    c_loc = _mm(v.T.astype(BF16), kp.astype(BF16)) * bd_ref[...]
    n_loc = jnp.sum(kp, axis=0, keepdims=True)

    inter_log = cum + m_s
    d_t = (log_i - cum).T
    yield
    row = lax.broadcasted_iota(jnp.int32, (BLOCK, BLOCK), 0)
    col = lax.broadcasted_iota(jnp.int32, (BLOCK, BLOCK), 1)
    s_parts, m_parts = [], []
    for h in range(N_HEADS):
        if h == N_HEADS // 2:
            yield
        t = h // 2
        cq = _spread_head(cum[:, BLOCK * t:BLOCK * (t + 1)], h)
        ilq = _spread_head(inter_log[:, BLOCK * t:BLOCK * (t + 1)], h)
        dk = jnp.broadcast_to(d_t[HEAD_DIM * h:HEAD_DIM * h + 1, :], (BLOCK, BLOCK))
        intra_log = jnp.where(row >= col, cq + dk, -jnp.inf)
        m_h = jnp.maximum(ilq, jnp.max(intra_log, axis=1, keepdims=True))
        s_parts.append(qk_all[:, BLOCK * h:BLOCK * (h + 1)] * jnp.exp(intra_log - m_h))
        m_parts.append(m_h)
    first = _lanes_in(BLOCK, 0, HEAD_DIM)
    m_t = jnp.concatenate([jnp.where(first, m_parts[0], m_parts[1]),
                           jnp.where(first, m_parts[2], m_parts[3])], axis=1)
    w_inter = jnp.exp(inter_log - m_t)
    s_b = jnp.concatenate(s_parts, axis=1).astype(BF16)
    num_intra, den_intra = _pair_apply(s_b, v_b, extra=onestack_ref[...])
    yield
    m_new = jnp.maximum(cum_last + m_s, a_end)
    s_prev = jnp.exp(cum_last + m_s - m_new)
    s_loc = jnp.exp(a_end - m_new)
    ct_ref[...] = s_prev * ct + s_loc * c_loc
    n_ref[...] = jnp.broadcast_to(s_prev * n_s + s_loc * n_loc, n_ref.shape)
    m_ref[...] = jnp.broadcast_to(m_new, m_ref.shape)
    num = num_intra + w_inter * q_ct
    den = den_intra + w_inter * q_n
    o_ref[...] = num / jnp.maximum(jnp.abs(den), jnp.exp(-m_t))


def _recurrent_kernel(ugla_ref, uml_ref, uret_ref, cw_ref, tril_ref, e3_ref, bdgla_ref, bd_ref, bdones_ref,
                      onestack_ref, dec_ref, qdec_ref, kdec_ref, gam_ref,
                      ogla_ref, oml_ref, oret_ref,
                      stgla_ref, xbuf_ref, ct_ref, n_ref, m_ref, stret_ref):
    @pl.when(pl.program_id(0) == 0)
    def _():
        for ref in (stgla_ref, xbuf_ref, ct_ref, n_ref, m_ref, stret_ref):
            ref[...] = jnp.zeros_like(ref)

    chains = []
    for b in range(ugla_ref.shape[0]):
        chains += [
            _ml_chunk(uml_ref.at[b], cw_ref, tril_ref, e3_ref, bd_ref, bdones_ref, onestack_ref,
                      oml_ref.at[b], xbuf_ref.at[b], ct_ref.at[b], n_ref.at[b], m_ref.at[b]),
            _gla_chunk(ugla_ref.at[b], tril_ref, bdgla_ref, ogla_ref.at[b], stgla_ref.at[b]),
            _ret_chunk(uret_ref.at[b], dec_ref, qdec_ref, kdec_ref, gam_ref, bd_ref, oret_ref.at[b],
                       stret_ref.at[b])]
    while chains:
        chains = [chain for chain in chains if next(chain, True) is None]


def _merge_kernel(hn_ref, h_ref, ysb_ref, ogla_ref, oml_ref, oret_ref, r_ref, opre_ref, gret_ref,
                  wm_ref, bm_ref, wb_ref, wo_ref, g_ref, ngla_ref, nml_ref, nret_ref, bdmean_ref, o_ref):
    hn = hn_ref[...]
    bdmean = bdmean_ref[...]
    r = r_ref[...]
    g = gret_ref[...]
    branches = (ysb_ref[...],
                _head_rmsnorm(ogla_ref[...], ngla_ref[...], bdmean) * (r * _sigmoid(r)),
                _sigmoid(opre_ref[...]) * _head_rmsnorm(oml_ref[...], nml_ref[...], bdmean),
                (g * _sigmoid(g)) * _head_rmsnorm(oret_ref[...], nret_ref[...], bdmean))
    merged = None
    for n, y in enumerate(branches):
        cols = slice(D_MODEL * n, D_MODEL * (n + 1))
        gate = _sigmoid(_mm_nt(hn, wm_ref[cols, :]) + bm_ref[:, cols])
        term = gate * _mm(y.astype(BF16), wb_ref[n].astype(BF16))
        merged = term if merged is None else merged + term
    mix_out = _mm(merged.astype(BF16), wo_ref[...].astype(BF16))
    o_ref[...] = h_ref[...] + _rmsnorm(mix_out, g_ref[...])


def _ffn_kernel(h_ref, gpre_ref, wg_ref, wu_ref, wd_ref, gpost_ref, o_ref, acc_ref):
    x = h_ref[...]
    f = _rmsnorm(x, gpre_ref[...]).astype(BF16)
    for c in range(D_FF // FF_CHUNK):
        cols = slice(FF_CHUNK * c, FF_CHUNK * (c + 1))
        a = _mm(f, wg_ref[:, cols].astype(BF16))
        act = ((a * _sigmoid(a)) * _mm(f, wu_ref[:, cols].astype(BF16))).astype(BF16)
        part = _mm(act, wd_ref[cols, :].astype(BF16))
        if c == 0:
            acc_ref[...] = part
        else:
            acc_ref[...] += part
    o_ref[...] = x + _rmsnorm(acc_ref[...], gpost_ref[...])


def _largest_tile(n, candidates):
    for tm in candidates:
        if n % tm == 0:
            return tm
    raise ValueError(f"{n} rows are not a multiple of {BLOCK}")


def _const_spec(shape, grid_rank):
    zeros = (0,) * len(shape)
    return pl.BlockSpec(shape, (lambda b, j: zeros) if grid_rank == 2 else (lambda n: zeros))


def _params(sem):
    return pltpu.CompilerParams(dimension_semantics=sem, vmem_limit_bytes=V7X_VMEM_LIMIT)


def _row_call(kernel, name, batch, lt, row_inputs, tile_inputs, const_inputs, out_widths, out_dtypes):
    tm = _largest_tile(lt, (640, 512, 384, 256, 128))
    nb = lt // tm
    rows = batch * lt
    row_spec = lambda w, cb: pl.BlockSpec((tm, w), lambda b, j: (b * nb + j, cb))
    return pl.pallas_call(
        kernel,
        grid=(batch, nb),
        in_specs=([row_spec(w, cb) for _, w, cb in row_inputs]
                  + [pl.BlockSpec((tm, a.shape[1]), lambda b, j: (j, 0)) for a in tile_inputs]
                  + [a[1] if isinstance(a, tuple) else _const_spec(a.shape, 2) for a in const_inputs]),
        out_specs=[row_spec(w, 0) for w in out_widths],
        out_shape=[jax.ShapeDtypeStruct((rows, w), dt) for w, dt in zip(out_widths, out_dtypes)],
        compiler_params=_params(("parallel", "parallel")),
        name=name,
    )(*[a for a, _, _ in row_inputs], *tile_inputs, *[a[0] if isinstance(a, tuple) else a for a in const_inputs])


def _w_in_group(w_in_t, layer, width, index):
    return (w_in_t, pl.BlockSpec((None, width, D_MODEL), lambda b, j: (layer, index, 0)))


def _layer(stacked, layer, single_buffer=False):
    zeros = (0,) * (stacked.ndim - 1)
    mode = pl.Buffered(1) if single_buffer else None
    return (stacked, pl.BlockSpec((None,) + stacked.shape[1:], lambda b, j: (layer,) + zeros, pipeline_mode=mode))


def _full(a):
    return (a, a.shape[1], 0)


def _ffn_call(h, batch, lt, consts, final_seq=None):
    if final_seq is None:
        tm, rows_per_batch, out_rows = _largest_tile(lt, (640, 512, 384, 256, 128)), lt, lt
    else:
        tm, rows_per_batch, out_rows = _largest_tile(final_seq, (512, 256, 128)), final_seq, final_seq
    spec = pl.BlockSpec((None, tm, D_MODEL), lambda b, j: (b, j, 0))
    return pl.pallas_call(
        _ffn_kernel,
        grid=(batch, rows_per_batch // tm),
        in_specs=[spec] + [a[1] if isinstance(a, tuple) else _const_spec(a.shape, 2) for a in consts],
        out_specs=spec,
        out_shape=jax.ShapeDtypeStruct((batch, out_rows, D_MODEL), F32),
        scratch_shapes=[pltpu.VMEM((tm, D_MODEL), F32)],
        compiler_params=_params(("parallel", "parallel")),
        name="swiglu",
    )(h.reshape(batch, lt, D_MODEL), *[a[0] if isinstance(a, tuple) else a for a in consts])


def _logical_to_memory_block(n, nc):
    return jnp.where(n == 0, nc - 1, n - 1)


def _recurrent_call(batch, lt, u_list, const_inputs):
    nc = lt // BLOCK
    u3 = [u.reshape(batch, lt, u.shape[1]) for u in u_list]
    chunk_map = lambda n: (0, _logical_to_memory_block(n, nc), 0)
    o_spec = pl.BlockSpec((batch, BLOCK, BW), chunk_map)
    outs = pl.pallas_call(
        _recurrent_kernel,
        grid=(nc,),
        in_specs=([pl.BlockSpec((batch, BLOCK, u.shape[2]), chunk_map) for u in u3]
                  + [_const_spec(a.shape, 1) for a in const_inputs]),
        out_specs=[o_spec] * 3,
        out_shape=[jax.ShapeDtypeStruct((batch, lt, BW), F32)] * 3,
        scratch_shapes=[pltpu.VMEM((batch, BW, BLOCK), F32),
                        pltpu.VMEM((batch, BLOCK + 8, 2 * BW), F32),
                        pltpu.VMEM((batch, BW, BW), F32),
                        pltpu.VMEM((batch, 8, BW), F32),
                        pltpu.VMEM((batch, 8, BW), F32),
                        pltpu.VMEM((batch, BW, BW), F32)],
        compiler_params=_params(("arbitrary",)),
        name="recurrent_mixers",
    )(*u3, *const_inputs)
    return [o.reshape(batch * lt, BW) for o in outs]


def _sb_call(u_sb, nuo, batch, lt):
    nc = lt // BLOCK
    u3 = u_sb.reshape(batch, lt, U_SB)
    q_spec = pl.BlockSpec((batch, BLOCK, BW), lambda i: (0, _logical_to_memory_block(i, nc), 0))
    return pl.pallas_call(
        _sb_kernel,
        grid=(nc,),
        in_specs=[q_spec,
                  pl.BlockSpec((batch, lt, BW), lambda i: (0, 0, 1)),
                  pl.BlockSpec((batch, lt, BW), lambda i: (0, 0, 2)),
                  _const_spec(nuo.shape, 1)],
        out_specs=q_spec,
        out_shape=jax.ShapeDtypeStruct((batch, lt, BW), F32),
        scratch_shapes=[pltpu.VMEM((batch, BLOCK, BW), F32), pltpu.VMEM((batch, BLOCK, N_HEADS * BLOCK), F32)],
        compiler_params=_params(("parallel",)),
        name="stick_breaking",
    )(u3, u3, u3, nuo).reshape(batch * lt, BW)


def _constants(seq):
    idx = jnp.arange(BLOCK)
    tril = (idx[:, None] >= idx[None, :])
    nuo = -jnp.concatenate([tril, jnp.ones((BLOCK, BLOCK), bool)], axis=1).astype(BF16)
    c = {"tril": jnp.concatenate([tril, tril], axis=1).astype(BF16),
         "nuo": jnp.concatenate([nuo, nuo], axis=0)}
    head_e = jnp.arange(BW) // HEAD_DIM
    c["bd_gla"] = (head_e[:, None] == (jnp.arange(BLOCK) // GLA_DK)[None, :]).astype(F32)
    bd = head_e[:, None] == head_e[None, :]
    c["bd"] = bd.astype(F32)
    c["bdones"] = bd.astype(BF16)
    c["bdmean"] = (bd.astype(F32) / HEAD_DIM).astype(BF16)
    c["onestack"] = ((jnp.arange(2 * BLOCK) // BLOCK)[:, None]
                     == (jnp.arange(BLOCK) // HEAD_DIM)[None, :]).astype(BF16)
    lane = jnp.arange(BLOCK)[:, None]
    spread = jnp.concatenate([lane == I_LANE + head_e[None, :], lane == F_LANE + head_e[None, :]], axis=1)
    c["e3"] = jnp.concatenate([spread] * 3, axis=0).astype(BF16)
    positions = jnp.concatenate([jnp.arange(N_META, seq + N_META), jnp.arange(-N_FRONT_PAD, N_META)])
    half = HEAD_DIM // 2
    inv_freq = ROPE_BASE ** (-jnp.arange(half, dtype=F32) / half)
    ang = positions.astype(jnp.int32).astype(F32)[:, None] * inv_freq[None, :]
    cos, sin = jnp.cos(ang), jnp.sin(ang)
    c["cos"] = jnp.tile(jnp.concatenate([cos, cos], axis=1), (1, N_HEADS))
    c["sin"] = jnp.tile(jnp.concatenate([-sin, sin], axis=1), (1, N_HEADS))
    log_gamma = jnp.log1p(-jnp.exp2(-5.0 - jnp.arange(N_HEADS, dtype=F32)))
    pos = jnp.arange(BLOCK, dtype=F32)
    diff = jnp.maximum(pos[:, None] - pos[None, :], 0.0)
    dec = jnp.where(tril, jnp.exp(log_gamma[:, None, None] * diff), 0.0)
    c["ret_dec"] = jnp.moveaxis(dec, 0, 1).reshape(BLOCK, N_HEADS * BLOCK)
    c["ret_qdec"] = jnp.repeat(jnp.exp(log_gamma[:, None] * (pos + 1.0)).T, HEAD_DIM, axis=1)
    c["ret_kdec"] = jnp.repeat(jnp.exp(log_gamma[:, None] * (BLOCK - 1.0 - pos)).T, HEAD_DIM, axis=1)
    c["ret_gam"] = jnp.repeat(jnp.exp(log_gamma * BLOCK), HEAD_DIM)[None, :]
    return c


def _layout_weights(w_in_t):
    rows = lambda name: w_in_t[:, _C[name][0]:_C[name][1], :].astype(BF16)
    pad = jnp.zeros((w_in_t.shape[0], BLOCK - GLA_LOW_RANK - 2 * N_HEADS, D_MODEL), BF16)
    return {
        "ml": jnp.concatenate([rows(n) for n in ("ml_qk", "ml_v", "ml_o")], axis=1),
        "ret": rows("ret"),
        "small": jnp.concatenate([rows("gla_code"), rows("ml_i"), rows("ml_f"), pad], axis=1),
        "merge": rows("merge"),
    }


def kernel(x, meta_tokens, norm_mix_pre, norm_mix_post, norm_ffn_pre, norm_ffn_post, w_in,
           gla_w_gate_up, gla_b_gate, gla_norm, ml_conv, ml_b_i, ml_b_f, ml_norm, ret_norm,
           w_branch, b_merge, w_out, ffn_w_gate, ffn_w_up, ffn_w_down):
    batch, seq, d = x.shape
    depth = w_in.shape[0]
    lt = seq + BLOCK
    pad = jnp.zeros((batch, N_FRONT_PAD, d), x.dtype)
    meta = jnp.broadcast_to(meta_tokens.astype(x.dtype)[None], (batch, N_META, d))
    h = jnp.concatenate([x, pad, meta], axis=1).reshape(batch * lt, d)

    c = _constants(seq)
    w_in_t = jnp.swapaxes(w_in, 1, 2)
    w = _layout_weights(w_in_t)
    wg_gla = jnp.pad(gla_w_gate_up, ((0, 0), (0, BLOCK - GLA_LOW_RANK), (0, 0))).astype(BF16)
    tail = BLOCK - GLA_LOW_RANK - 2 * N_HEADS
    b_gate = jnp.pad(jnp.concatenate([ml_b_i, ml_b_f], axis=-1), ((0, 0), (GLA_LOW_RANK, tail)))
    wb, wo = w_branch, w_out
    wfg, wfu, wfd = ffn_w_gate, ffn_w_up, ffn_w_down
    tm = _largest_tile(lt, (640, 512, 384, 256, 128))

    for l in range(depth):
        hn, u_sb, u_gla, u_ml, u_ret = _row_call(
            functools.partial(_proj_kernel, tm=tm, seq=seq), "in_projection", batch, lt,
            [_full(h)], [c["cos"], c["sin"]],
            [norm_mix_pre[l][None], _w_in_group(w_in_t, l, U_SB, 0), _w_in_group(w_in_t, l, U_SB, 1),
             _layer(w["ml"], l), _layer(w["ret"], l), _layer(w["small"], l), _layer(wg_gla, l),
             gla_b_gate[l][None], b_gate[l][None]],
            [D_MODEL, U_SB, U_GLA, U_ML, U_RET], [BF16, BF16, F32, F32, F32])
        y_sb = _sb_call(u_sb, c["nuo"], batch, lt)
        o_gla, o_ml, o_ret = _recurrent_call(
            batch, lt, [u_gla, u_ml, u_ret],
            [ml_conv[l], c["tril"], c["e3"], c["bd_gla"], c["bd"], c["bdones"], c["onestack"],
             c["ret_dec"], c["ret_qdec"], c["ret_kdec"], c["ret_gam"]])
        (h,) = _row_call(
            _merge_kernel, "merge", batch, lt,
            [_full(hn), _full(h), _full(y_sb), _full(o_gla), _full(o_ml), _full(o_ret),
             (u_gla, BW, 2), (u_ml, BW, 3), (u_ret, BW, 3)], [],
            [_layer(w["merge"], l), b_merge[l].reshape(1, N_BRANCH * D_MODEL), _layer(wb, l, True), _layer(wo, l, True),
             norm_mix_post[l][None],
             gla_norm[l][None], ml_norm[l][None], ret_norm[l][None], c["bdmean"]],
            [D_MODEL], [F32])
        ffn_consts = [norm_ffn_pre[l][None], _layer(wfg, l, True), _layer(wfu, l, True), _layer(wfd, l, True),
                      norm_ffn_post[l][None]]
        if l + 1 < depth:
            h = _ffn_call(h, batch, lt, ffn_consts).reshape(batch * lt, d)
        else:
            return _ffn_call(h, batch, lt, ffn_consts, final_seq=seq)
```

```python
import functools

import jax
import jax.numpy as jnp
from jax import lax
from jax.experimental import pallas as pl
from jax.experimental.pallas import tpu as pltpu

F32 = jnp.float32
BF16 = jnp.bfloat16

D_MODEL = 1024
BLOCK = 128
HEAD_DIM = 64
N_HEADS = 4
BW = N_HEADS * HEAD_DIM
N_BRANCH = 4
N_META = 16
N_FRONT_PAD = BLOCK - N_META
GLA_DK = 32
GLA_LOW_RANK = 16
GLA_GATE_NORMALIZER = 16.0
CONV_WIDTH = 4
D_FF = 2816
FF_CHUNK = 256
MERGE_CHUNK = 256
ROPE_BASE = 10000.0
NORM_EPS = 1e-6
SB_LOG_ZERO = -104.0
SB_STATIC_TILES = 3
V7X_VMEM_LIMIT = 56 * 1024 * 1024

CODE_LANE = 0
I_LANE = GLA_LOW_RANK
F_LANE = GLA_LOW_RANK + N_HEADS

_C = {}
_off = 0
for _name, _w in (("sb", 768), ("gla_q", 128), ("gla_k", 128), ("gla_v", 256), ("gla_r", 256), ("gla_code", 16),
                  ("ml_qk", 512), ("ml_v", 256), ("ml_i", 4), ("ml_f", 4), ("ml_o", 256), ("ret", 1024),
                  ("merge", 4096)):
    _C[_name] = (_off, _off + _w)
    _off += _w
IN_WIDTH = _off

U_SB, U_GLA, U_ML, U_RET = 768, 896, 1152, 1024


def _mm(a, b):
    return jnp.dot(a, b, preferred_element_type=F32)


def _mm_nt(a, b):
    return lax.dot_general(a, b, (((1,), (1,)), ((), ())), preferred_element_type=F32)


def _split(x):
    hi = x.astype(BF16)
    lo = (x - hi.astype(F32)).astype(BF16)
    return hi, lo


def _mm_split_r(c2, x):
    hi, lo = _split(x)
    return _mm(c2, jnp.concatenate([hi, lo], axis=0))


def _softplus(x):
    return jnp.maximum(x, 0.0) + jnp.log(1.0 + jnp.exp(-jnp.abs(x)))


def _log_sigmoid(x):
    return -_softplus(-x)


def _sigmoid(x):
    return 1.0 / (1.0 + jnp.exp(-x))


def _rmsnorm(x, g):
    return x * lax.rsqrt(jnp.mean(x * x, axis=-1, keepdims=True) + NORM_EPS) * g


def _head_rmsnorm(y, g, bdmean):
    ms = _mm((y * y).astype(BF16), bdmean)
    return y * lax.rsqrt(ms + NORM_EPS) * g


def _lanes_in(width, lo, hi):
    lane = lax.broadcasted_iota(jnp.int32, (1, width), 1)
    return (lane >= lo) & (lane < hi)


def _row_stack_heads(x, per_head, n=N_HEADS):
    keep = [_lanes_in(x.shape[1], per_head * h, per_head * (h + 1)).astype(x.dtype) for h in range(n)]
    return jnp.concatenate([x * m for m in keep], axis=0)


def _pair_scores(q_b, k_b):
    zs = []
    for p in range(N_HEADS // 2):
        lanes = slice(BLOCK * p, BLOCK * (p + 1))
        zs.append(_mm_nt(q_b[:, lanes], _row_stack_heads(k_b[:, lanes], HEAD_DIM, 2)))
    return jnp.concatenate(zs, axis=1)


def _pair_apply(s_b, v_b, extra=None):
    outs, extras = [], []
    for p in range(N_HEADS // 2):
        rhs = _row_stack_heads(v_b[:, BLOCK * p:BLOCK * (p + 1)], HEAD_DIM, 2)
        if extra is not None:
            rhs = jnp.concatenate([rhs, extra], axis=1)
        r = _mm(s_b[:, 2 * BLOCK * p:2 * BLOCK * (p + 1)], rhs)
        outs.append(r[:, :BLOCK])
        extras.append(r[:, BLOCK:])
    out = jnp.concatenate(outs, axis=1)
    return out if extra is None else (out, jnp.concatenate(extras, axis=1))


def _tile4(x):
    return jnp.concatenate([x] * N_HEADS, axis=1)


def _rotary(x, cos, sin_signed):
    first_half = _lanes_in(BLOCK, 0, 32) | _lanes_in(BLOCK, 64, 96)
    outs = []
    for t in range(2):
        lanes = slice(BLOCK * t, BLOCK * (t + 1))
        xt = x[:, lanes]
        swapped = jnp.where(first_half, pltpu.roll(xt, 96, 1), pltpu.roll(xt, 32, 1))
        outs.append(xt * cos[:, lanes] + swapped * sin_signed[:, lanes])
    return jnp.concatenate(outs, axis=1)


def _proj_kernel(h_ref, cos_ref, sin_ref, g_ref, wsb_ref, wgla_ref, wmlqk_ref, wmlv_ref, wmlo_ref, wret_ref,
                 wsmall_ref, wg_ref, bg_ref, bgate_ref, hn_ref, usb_ref, ugla_ref, uml_ref, uret_ref, *, tm, seq):
    x = h_ref[...]
    pos = pl.program_id(1) * tm + lax.broadcasted_iota(jnp.int32, (tm, 1), 0)
    valid = ((pos < seq) | (pos >= seq + N_FRONT_PAD)).astype(F32)
    hn = (_rmsnorm(x, g_ref[...]) * valid).astype(BF16)
    hn_ref[...] = hn
    proj = lambda w_ref: _mm_nt(hn, w_ref[0].astype(BF16))
    small = _mm_nt(hn, wsmall_ref[...])
    ret = proj(wret_ref)
    log_a_pre = _mm(small.astype(BF16), wg_ref[...])
    usb_ref[...] = proj(wsb_ref).astype(BF16)
    cos = cos_ref[...]
    sin = sin_ref[...]
    uret_ref[:, 0:256] = _rotary(ret[:, 0:256], cos, sin)
    uret_ref[:, 256:512] = _rotary(ret[:, 256:512], cos, sin) * (HEAD_DIM ** -0.5)
    uret_ref[:, 512:1024] = ret[:, 512:1024]
    ugla_ref[:, 768:896] = _log_sigmoid(log_a_pre + bg_ref[...]) * (1.0 / GLA_GATE_NORMALIZER)
    gates = small + bgate_ref[...]
    uml_ref[:, 1024:1152] = jnp.where(_lanes_in(BLOCK, F_LANE, F_LANE + N_HEADS), _log_sigmoid(gates), gates)
    ugla_ref[:, 0:768] = proj(wgla_ref)
    uml_ref[:, 0:512] = proj(wmlqk_ref)
    uml_ref[:, 512:768] = proj(wmlv_ref)
    uml_ref[:, 768:1024] = proj(wmlo_ref)


def _sb_tile(q, k_ref, v_ref, j, n_blocks, bias, nuo2, cb):
    state = {"cb": cb, "out": None}
    for _ in _sb_tile_stages(q, k_ref, v_ref, j, n_blocks, bias, nuo2, state):
        pass
    return state["out"], state["cb"]


def _sb_tile_stages(q, k_ref, v_ref, j, n_blocks, bias, nuo2, state):
    start = pl.multiple_of(jnp.where(j <= 0, n_blocks - 1, j - 1) * BLOCK, BLOCK)
    kj = k_ref[pl.ds(start, BLOCK), :]
    zs = []
    for p in range(N_HEADS // 2):
        lanes = slice(BLOCK * p, BLOCK * (p + 1))
        k_rows = _row_stack_heads(kj[:, lanes], HEAD_DIM, 2)
        zs.append(_mm_nt(q[:, lanes], k_rows))
    yield
    z = jnp.concatenate(zs, axis=1) + bias
    hi, lo = _split(_softplus(z))
    rs = []
    for h in range(N_HEADS):
        lanes = slice(BLOCK * h, BLOCK * (h + 1))
        rs.append(_mm(jnp.concatenate([hi[:, lanes], lo[:, lanes]], axis=1), nuo2))
    yield
    cb = state["cb"]
    a = jnp.exp(z + jnp.concatenate([r[:, :BLOCK] for r in rs], axis=1) + cb).astype(BF16)
    state["cb"] = cb + jnp.concatenate([r[:, BLOCK:] for r in rs], axis=1)
    contrib = _mm(a, _row_stack_heads(v_ref[pl.ds(start, BLOCK), :], HEAD_DIM))
    yield
    acc_ref = state.get("acc_ref")
    if acc_ref is None:
        state["out"] = contrib if state["out"] is None else state["out"] + contrib
    elif state["out"] is None:
        acc_ref[...] = contrib
        state["out"] = acc_ref
    else:
        acc_ref[...] += contrib
    yield


def _sb_kernel(q_ref, k_ref, v_ref, nuo_ref, o_ref, acc_ref, cb_ref):
    i = pl.program_id(0)
    n_blocks = pl.num_programs(0)
    batch = q_ref.shape[0]
    qs = [q_ref[b] * jnp.asarray(HEAD_DIM ** -0.5, BF16) for b in range(batch)]
    row = lax.broadcasted_iota(jnp.int32, (BLOCK, BLOCK), 0)
    col = lax.broadcasted_iota(jnp.int32, (BLOCK, BLOCK), 1)
    nuo2 = nuo_ref[...]

    def bias_for(j, causal):
        ok = (j * BLOCK + col) >= N_FRONT_PAD
        if causal:
            ok = ok & (col < row)
        return jnp.concatenate([jnp.where(ok, 0.0, -jnp.inf)] * N_HEADS, axis=1)

    def all_max(x):
        return jnp.max(jnp.max(x, axis=1, keepdims=True), axis=0, keepdims=True)[0, 0]

    biases = [bias_for(i - t, t == 0) for t in range(SB_STATIC_TILES)]
    states = [{"cb": jnp.zeros((BLOCK, N_HEADS * BLOCK), F32), "out": None, "acc_ref": acc_ref.at[b]}
              for b in range(batch)]
    tiles = [_sb_tile_stages(qs[b], k_ref.at[b], v_ref.at[b], i - t, n_blocks, biases[t], nuo2, states[b])
             for t in range(SB_STATIC_TILES) for b in range(batch)]
    for _ in range(4):
        for tile in tiles:
            next(tile)

    def cond(c):
        j, mx = c
        return jnp.logical_and(j >= 0, mx > SB_LOG_ZERO)

    for b in range(batch):
        cb_ref[b] = states[b]["cb"]

        def body(c, b=b):
            j, _ = c
            contrib, cbn = _sb_tile(qs[b], k_ref.at[b], v_ref.at[b], j, n_blocks, bias_for(j, False), nuo2,
                                    cb_ref[b])
            acc_ref[b] += contrib
            cb_ref[b] = cbn
            return j - 1, all_max(cbn)

        lax.while_loop(cond, body, (i - SB_STATIC_TILES, all_max(states[b]["cb"])))
        o_ref[b] = acc_ref[b]


def _gla_chunk(u_ref, tril_ref, bd_ref, o_ref, st_ref):
    q = u_ref[:, 0:128] * (GLA_DK ** -0.5)
    k = u_ref[:, 128:256]
    v = u_ref[:, 256:512]
    log_a = u_ref[:, 768:896]
    cum = _mm_split_r(tril_ref[...], log_a)
    yield
    cum_last = cum[BLOCK - 1:BLOCK, :]
    qd = (q * jnp.exp(cum)).astype(BF16)
    kd = (k * jnp.exp(-cum)).astype(BF16)
    ke = (k * jnp.exp(cum_last - cum)).astype(BF16)
    st = st_ref[...]
    inter = _mm_nt(qd, st.astype(BF16))
    scores = _mm_nt(qd, _row_stack_heads(kd, GLA_DK))
    local = _mm(v.T.astype(BF16), ke)
    yield
    row = lax.broadcasted_iota(jnp.int32, (BLOCK, BLOCK), 0)
    col = lax.broadcasted_iota(jnp.int32, (BLOCK, BLOCK), 1)
    s = jnp.where(_tile4(row >= col), scores, 0.0)
    intra = _pair_apply(s.astype(BF16), v.astype(BF16))
    yield
    st_ref[...] = st * jnp.exp(cum_last) + local * bd_ref[...]
    o_ref[...] = intra + inter


def _ret_chunk(u_ref, dec_ref, qdec_ref, kdec_ref, gam_ref, bd_ref, o_ref, st_ref):
    qr = u_ref[:, 0:256]
    kr = u_ref[:, 256:512]
    v = u_ref[:, 512:768]
    scores = _pair_scores(qr.astype(BF16), kr.astype(BF16))
    st = st_ref[...]
    inter = _mm_nt((qr * qdec_ref[...]).astype(BF16), st.astype(BF16))
    local = _mm(v.T.astype(BF16), (kr * kdec_ref[...]).astype(BF16))
    yield
    s = scores * dec_ref[...]
    intra = _pair_apply(s.astype(BF16), v.astype(BF16))
    yield
    st_ref[...] = st * gam_ref[...] + local * bd_ref[...]
    o_ref[...] = intra + inter


def _spread_head(x_t, h):
    own = _lanes_in(BLOCK, 0, HEAD_DIM) if h % 2 == 0 else _lanes_in(BLOCK, HEAD_DIM, BLOCK)
    return jnp.where(own, x_t, pltpu.roll(x_t, HEAD_DIM, 1))


def _ml_chunk(u_ref, cw_ref, tril_ref, e3_ref, bd_ref, bdones_ref, onestack_ref, o_ref, xbuf_ref, ct_ref, n_ref,
              m_ref):
    gates = u_ref[:, 1024:1152]
    gates_cum = _mm_split_r(tril_ref[...], gates)
    xbuf_ref[8:8 + BLOCK, :] = u_ref[:, 0:512]
    conv = None
    for j in range(CONV_WIDTH):
        term = cw_ref[j:j + 1, :] * xbuf_ref[8 - (CONV_WIDTH - 1) + j:8 - (CONV_WIDTH - 1) + j + BLOCK, :]
        conv = term if conv is None else conv + term
    xbuf_ref[0:8, :] = xbuf_ref[BLOCK:BLOCK + 8, :]
    yield
    compact = jnp.where(_lanes_in(BLOCK, F_LANE, F_LANE + N_HEADS), gates_cum, gates)
    t1 = compact.astype(BF16)
    rest = compact - t1.astype(F32)
    t2 = rest.astype(BF16)
    t3 = (rest - t2.astype(F32)).astype(BF16)
    spread = _mm(jnp.concatenate([t1, t2, t3], axis=1), e3_ref[...])
    qk = conv * _sigmoid(conv)
    q = qk[:, 0:BW]
    k = qk[:, BW:2 * BW] * (HEAD_DIM ** -0.5)
    v = u_ref[:, 512:768]
    q_b = q.astype(BF16)
    k_b = k.astype(BF16)
    qk_all = _pair_scores(q_b, k_b)
    m_s = m_ref[0:1, :]
    n_s = n_ref[0:1, :]
    ct = ct_ref[...]
    q_ct = _mm_nt(q_b, ct.astype(BF16))
    q_n = _mm((q * n_s).astype(BF16), bdones_ref[...])
    yield
    log_i = spread[:, 0:BW]
    cum = spread[:, BW:2 * BW]

    cum_last = cum[BLOCK - 1:BLOCK, :]
    w_end = cum_last - cum + log_i
    a_end = jnp.max(w_end, axis=0, keepdims=True)
    kp = k * jnp.exp(w_end - a_end)
    v_b = v.astype(BF16)
    c_loc = _mm(v.T.astype(BF16), kp.astype(BF16)) * bd_ref[...]
    n_loc = jnp.sum(kp, axis=0, keepdims=True)

    inter_log = cum + m_s
    d_t = (log_i - cum).T
    yield
    row = lax.broadcasted_iota(jnp.int32, (BLOCK, BLOCK), 0)
    col = lax.broadcasted_iota(jnp.int32, (BLOCK, BLOCK), 1)
    s_parts, m_parts = [], []
    for h in range(N_HEADS):
        if h == N_HEADS // 2:
            yield
        t = h // 2
        cq = _spread_head(cum[:, BLOCK * t:BLOCK * (t + 1)], h)
        ilq = _spread_head(inter_log[:, BLOCK * t:BLOCK * (t + 1)], h)
        dk = jnp.broadcast_to(d_t[HEAD_DIM * h:HEAD_DIM * h + 1, :], (BLOCK, BLOCK))
        intra_log = jnp.where(row >= col, cq + dk, -jnp.inf)
        m_h = jnp.maximum(ilq, jnp.max(intra_log, axis=1, keepdims=True))
        s_parts.append(qk_all[:, BLOCK * h:BLOCK * (h + 1)] * jnp.exp(intra_log - m_h))
        m_parts.append(m_h)
    first = _lanes_in(BLOCK, 0, HEAD_DIM)
    m_t = jnp.concatenate([jnp.where(first, m_parts[0], m_parts[1]),
                           jnp.where(first, m_parts[2], m_parts[3])], axis=1)
    w_inter = jnp.exp(inter_log - m_t)
    s_b = jnp.concatenate(s_parts, axis=1).astype(BF16)
    num_intra, den_intra = _pair_apply(s_b, v_b, extra=onestack_ref[...])
    yield
    m_new = jnp.maximum(cum_last + m_s, a_end)
    s_prev = jnp.exp(cum_last + m_s - m_new)
    s_loc = jnp.exp(a_end - m_new)
    ct_ref[...] = s_prev * ct + s_loc * c_loc
    n_ref[...] = jnp.broadcast_to(s_prev * n_s + s_loc * n_loc, n_ref.shape)
    m_ref[...] = jnp.broadcast_to(m_new, m_ref.shape)
    num = num_intra + w_inter * q_ct
    den = den_intra + w_inter * q_n
    o_ref[...] = num / jnp.maximum(jnp.abs(den), jnp.exp(-m_t))


def _recurrent_kernel(ugla_ref, uml_ref, uret_ref, cw_ref, tril_ref, e3_ref, bdgla_ref, bd_ref, bdones_ref,
                      onestack_ref, dec_ref, qdec_ref, kdec_ref, gam_ref,
                      ogla_ref, oml_ref, oret_ref,
                      stgla_ref, xbuf_ref, ct_ref, n_ref, m_ref, stret_ref):
    @pl.when(pl.program_id(0) == 0)
    def _():
        for ref in (stgla_ref, xbuf_ref, ct_ref, n_ref, m_ref, stret_ref):
            ref[...] = jnp.zeros_like(ref)

    chains = []
    for b in range(ugla_ref.shape[0]):
        chains += [
            _ml_chunk(uml_ref.at[b], cw_ref, tril_ref, e3_ref, bd_ref, bdones_ref, onestack_ref,
                      oml_ref.at[b], xbuf_ref.at[b], ct_ref.at[b], n_ref.at[b], m_ref.at[b]),
            _gla_chunk(ugla_ref.at[b], tril_ref, bdgla_ref, ogla_ref.at[b], stgla_ref.at[b]),
            _ret_chunk(uret_ref.at[b], dec_ref, qdec_ref, kdec_ref, gam_ref, bd_ref, oret_ref.at[b],
                       stret_ref.at[b])]
    while chains:
        chains = [chain for chain in chains if next(chain, True) is None]


def _merge_kernel(hn_ref, h_ref, ysb_ref, ogla_ref, oml_ref, oret_ref, r_ref, opre_ref, gret_ref,
                  wm_ref, bm_ref, wb_ref, wo_ref, g_ref, ngla_ref, nml_ref, nret_ref, bdmean_ref, o_ref, acc_ref):
    hn = hn_ref[...]
    bdmean = bdmean_ref[...]
    r = r_ref[...]
    g = gret_ref[...]
    branches = [y.astype(BF16) for y in (
        ysb_ref[...],
        _head_rmsnorm(ogla_ref[...], ngla_ref[...], bdmean) * (r * _sigmoid(r)),
        _sigmoid(opre_ref[...]) * _head_rmsnorm(oml_ref[...], nml_ref[...], bdmean),
        (g * _sigmoid(g)) * _head_rmsnorm(oret_ref[...], nret_ref[...], bdmean))]
    for c in range(D_MODEL // MERGE_CHUNK):
        out_cols = slice(MERGE_CHUNK * c, MERGE_CHUNK * (c + 1))
        merged = None
        for n, y_b in enumerate(branches):
            cols = slice(D_MODEL * n + MERGE_CHUNK * c, D_MODEL * n + MERGE_CHUNK * (c + 1))
            gate = _sigmoid(_mm_nt(hn, wm_ref[0, cols, :].astype(BF16)) + bm_ref[:, cols])
            term = gate * _mm(y_b, wb_ref[n, :, out_cols].astype(BF16))
            merged = term if merged is None else merged + term
        part = _mm(merged.astype(BF16), wo_ref[out_cols, :].astype(BF16))
        if c == 0:
            acc_ref[...] = part
        else:
            acc_ref[...] += part
    o_ref[...] = h_ref[...] + _rmsnorm(acc_ref[...], g_ref[...])


def _ffn_kernel(h_ref, gpre_ref, wg_ref, wu_ref, wd_ref, gpost_ref, o_ref, acc_ref):
    x = h_ref[...]
    f = _rmsnorm(x, gpre_ref[...]).astype(BF16)
    for c in range(D_FF // FF_CHUNK):
        cols = slice(FF_CHUNK * c, FF_CHUNK * (c + 1))
        a = _mm(f, wg_ref[:, cols].astype(BF16))
        act = ((a * _sigmoid(a)) * _mm(f, wu_ref[:, cols].astype(BF16))).astype(BF16)
        part = _mm(act, wd_ref[cols, :].astype(BF16))
        if c == 0:
            acc_ref[...] = part
        else:
            acc_ref[...] += part
    o_ref[...] = x + _rmsnorm(acc_ref[...], gpost_ref[...])


def _largest_tile(n, candidates):
    for tm in candidates:
        if n % tm == 0:
            return tm
    raise ValueError(f"{n} rows are not a multiple of {BLOCK}")


def _const_spec(shape, grid_rank):
    zeros = (0,) * len(shape)
    return pl.BlockSpec(shape, (lambda b, j: zeros) if grid_rank == 2 else (lambda n: zeros))


def _params(sem):
    return pltpu.CompilerParams(dimension_semantics=sem, vmem_limit_bytes=V7X_VMEM_LIMIT)


def _row_call(kernel, name, batch, lt, row_inputs, tile_inputs, const_inputs, out_widths, out_dtypes,
              acc_width=None):
    tm = _largest_tile(lt, (640, 512, 384, 256, 128))
    nb = lt // tm
    rows = batch * lt
    row_spec = lambda w, cb: pl.BlockSpec((tm, w), lambda b, j: (b * nb + j, cb))
    return pl.pallas_call(
        kernel,
        grid=(batch, nb),
        in_specs=([row_spec(w, cb) for _, w, cb in row_inputs]
                  + [pl.BlockSpec((tm, a.shape[1]), lambda b, j: (j, 0)) for a in tile_inputs]
                  + [a[1] if isinstance(a, tuple) else _const_spec(a.shape, 2) for a in const_inputs]),
        out_specs=[row_spec(w, 0) for w in out_widths],
        out_shape=[jax.ShapeDtypeStruct((rows, w), dt) for w, dt in zip(out_widths, out_dtypes)],
        scratch_shapes=[] if acc_width is None else [pltpu.VMEM((tm, acc_width), F32)],
        compiler_params=_params(("parallel", "parallel")),
        name=name,
    )(*[a for a, _, _ in row_inputs], *tile_inputs, *[a[0] if isinstance(a, tuple) else a for a in const_inputs])


def _w_in_rows(w_in_t, layer, first, last=None):
    start, stop = _C[first][0], _C[last or first][1]
    block = (pl.Element(1), pl.Element(stop - start), pl.Element(D_MODEL))
    return (w_in_t, pl.BlockSpec(block, lambda b, j: (layer, start, 0), pipeline_mode=pl.Buffered(1)))


def _layer(stacked, layer, single_buffer=False):
    zeros = (0,) * (stacked.ndim - 1)
    mode = pl.Buffered(1) if single_buffer else None
    return (stacked, pl.BlockSpec((None,) + stacked.shape[1:], lambda b, j: (layer,) + zeros, pipeline_mode=mode))


def _full(a):
    return (a, a.shape[1], 0)


def _ffn_call(h, batch, lt, consts, final_seq=None):
    if final_seq is None:
        tm, rows_per_batch, out_rows = _largest_tile(lt, (640, 512, 384, 256, 128)), lt, lt
    else:
        tm, rows_per_batch, out_rows = _largest_tile(final_seq, (512, 256, 128)), final_seq, final_seq
    spec = pl.BlockSpec((None, tm, D_MODEL), lambda b, j: (b, j, 0))
    return pl.pallas_call(
        _ffn_kernel,
        grid=(batch, rows_per_batch // tm),
        in_specs=[spec] + [a[1] if isinstance(a, tuple) else _const_spec(a.shape, 2) for a in consts],
        out_specs=spec,
        out_shape=jax.ShapeDtypeStruct((batch, out_rows, D_MODEL), F32),
        scratch_shapes=[pltpu.VMEM((tm, D_MODEL), F32)],
        compiler_params=_params(("parallel", "parallel")),
        name="swiglu",
    )(h.reshape(batch, lt, D_MODEL), *[a[0] if isinstance(a, tuple) else a for a in consts])


def _logical_to_memory_block(n, nc):
    return jnp.where(n == 0, nc - 1, n - 1)


def _recurrent_call(batch, lt, u_list, const_inputs):
    nc = lt // BLOCK
    u3 = [u.reshape(batch, lt, u.shape[1]) for u in u_list]
    chunk_map = lambda n: (0, _logical_to_memory_block(n, nc), 0)
    o_spec = pl.BlockSpec((batch, BLOCK, BW), chunk_map)
    outs = pl.pallas_call(
        _recurrent_kernel,
        grid=(nc,),
        in_specs=([pl.BlockSpec((batch, BLOCK, u.shape[2]), chunk_map) for u in u3]
                  + [_const_spec(a.shape, 1) for a in const_inputs]),
        out_specs=[o_spec] * 3,
        out_shape=[jax.ShapeDtypeStruct((batch, lt, BW), F32)] * 3,
        scratch_shapes=[pltpu.VMEM((batch, BW, BLOCK), F32),
                        pltpu.VMEM((batch, BLOCK + 8, 2 * BW), F32),
                        pltpu.VMEM((batch, BW, BW), F32),
                        pltpu.VMEM((batch, 8, BW), F32),
                        pltpu.VMEM((batch, 8, BW), F32),
                        pltpu.VMEM((batch, BW, BW), F32)],
        compiler_params=_params(("arbitrary",)),
        name="recurrent_mixers",
    )(*u3, *const_inputs)
    return [o.reshape(batch * lt, BW) for o in outs]


def _sb_call(u_sb, nuo, batch, lt):
    nc = lt // BLOCK
    u3 = u_sb.reshape(batch, lt, U_SB)
    q_spec = pl.BlockSpec((batch, BLOCK, BW), lambda i: (0, _logical_to_memory_block(i, nc), 0))
    return pl.pallas_call(
        _sb_kernel,
        grid=(nc,),
        in_specs=[q_spec,
                  pl.BlockSpec((batch, lt, BW), lambda i: (0, 0, 1)),
                  pl.BlockSpec((batch, lt, BW), lambda i: (0, 0, 2)),
                  _const_spec(nuo.shape, 1)],
        out_specs=q_spec,
        out_shape=jax.ShapeDtypeStruct((batch, lt, BW), F32),
        scratch_shapes=[pltpu.VMEM((batch, BLOCK, BW), F32), pltpu.VMEM((batch, BLOCK, N_HEADS * BLOCK), F32)],
        compiler_params=_params(("parallel",)),
        name="stick_breaking",
    )(u3, u3, u3, nuo).reshape(batch * lt, BW)


def _constants(seq):
    idx = jnp.arange(BLOCK)
    tril = (idx[:, None] >= idx[None, :])
    nuo = -jnp.concatenate([tril, jnp.ones((BLOCK, BLOCK), bool)], axis=1).astype(BF16)
    c = {"tril": jnp.concatenate([tril, tril], axis=1).astype(BF16),
         "nuo": jnp.concatenate([nuo, nuo], axis=0)}
    head_e = jnp.arange(BW) // HEAD_DIM
    c["bd_gla"] = (head_e[:, None] == (jnp.arange(BLOCK) // GLA_DK)[None, :]).astype(F32)
    bd = head_e[:, None] == head_e[None, :]
    c["bd"] = bd.astype(F32)
    c["bdones"] = bd.astype(BF16)
    c["bdmean"] = (bd.astype(F32) / HEAD_DIM).astype(BF16)
    c["onestack"] = ((jnp.arange(2 * BLOCK) // BLOCK)[:, None]
                     == (jnp.arange(BLOCK) // HEAD_DIM)[None, :]).astype(BF16)
    lane = jnp.arange(BLOCK)[:, None]
    spread = jnp.concatenate([lane == I_LANE + head_e[None, :], lane == F_LANE + head_e[None, :]], axis=1)
    c["e3"] = jnp.concatenate([spread] * 3, axis=0).astype(BF16)
    positions = jnp.concatenate([jnp.arange(N_META, seq + N_META), jnp.arange(-N_FRONT_PAD, N_META)])
    half = HEAD_DIM // 2
    inv_freq = ROPE_BASE ** (-jnp.arange(half, dtype=F32) / half)
    ang = positions.astype(jnp.int32).astype(F32)[:, None] * inv_freq[None, :]
    cos, sin = jnp.cos(ang), jnp.sin(ang)
    c["cos"] = jnp.tile(jnp.concatenate([cos, cos], axis=1), (1, N_HEADS))
    c["sin"] = jnp.tile(jnp.concatenate([-sin, sin], axis=1), (1, N_HEADS))
    log_gamma = jnp.log1p(-jnp.exp2(-5.0 - jnp.arange(N_HEADS, dtype=F32)))
    pos = jnp.arange(BLOCK, dtype=F32)
    diff = jnp.maximum(pos[:, None] - pos[None, :], 0.0)
    dec = jnp.where(tril, jnp.exp(log_gamma[:, None, None] * diff), 0.0)
    c["ret_dec"] = jnp.moveaxis(dec, 0, 1).reshape(BLOCK, N_HEADS * BLOCK)
    c["ret_qdec"] = jnp.repeat(jnp.exp(log_gamma[:, None] * (pos + 1.0)).T, HEAD_DIM, axis=1)
    c["ret_kdec"] = jnp.repeat(jnp.exp(log_gamma[:, None] * (BLOCK - 1.0 - pos)).T, HEAD_DIM, axis=1)
    c["ret_gam"] = jnp.repeat(jnp.exp(log_gamma * BLOCK), HEAD_DIM)[None, :]
    return c


def _small_group(w_in_t):
    rows = lambda name: w_in_t[:, _C[name][0]:_C[name][1], :].astype(BF16)
    pad = jnp.zeros((w_in_t.shape[0], BLOCK - GLA_LOW_RANK - 2 * N_HEADS, D_MODEL), BF16)
    return jnp.concatenate([rows("gla_code"), rows("ml_i"), rows("ml_f"), pad], axis=1)


def kernel(x, meta_tokens, norm_mix_pre, norm_mix_post, norm_ffn_pre, norm_ffn_post, w_in,
           gla_w_gate_up, gla_b_gate, gla_norm, ml_conv, ml_b_i, ml_b_f, ml_norm, ret_norm,
           w_branch, b_merge, w_out, ffn_w_gate, ffn_w_up, ffn_w_down):
    batch, seq, d = x.shape
    depth = w_in.shape[0]
    lt = seq + BLOCK
    pad = jnp.zeros((batch, N_FRONT_PAD, d), x.dtype)
    meta = jnp.broadcast_to(meta_tokens.astype(x.dtype)[None], (batch, N_META, d))
    h = jnp.concatenate([x, pad, meta], axis=1).reshape(batch * lt, d)

    c = _constants(seq)
    w_in_t = jnp.swapaxes(w_in, 1, 2)
    w_small = _small_group(w_in_t)
    wg_gla = jnp.pad(gla_w_gate_up, ((0, 0), (0, BLOCK - GLA_LOW_RANK), (0, 0))).astype(BF16)
    tail = BLOCK - GLA_LOW_RANK - 2 * N_HEADS
    b_gate = jnp.pad(jnp.concatenate([ml_b_i, ml_b_f], axis=-1), ((0, 0), (GLA_LOW_RANK, tail)))
    wb, wo = w_branch, w_out
    wfg, wfu, wfd = ffn_w_gate, ffn_w_up, ffn_w_down
    tm = _largest_tile(lt, (640, 512, 384, 256, 128))

    for l in range(depth):
        hn, u_sb, u_gla, u_ml, u_ret = _row_call(
            functools.partial(_proj_kernel, tm=tm, seq=seq), "in_projection", batch, lt,
            [_full(h)], [c["cos"], c["sin"]],
            [norm_mix_pre[l][None], _w_in_rows(w_in_t, l, "sb"), _w_in_rows(w_in_t, l, "gla_q", "gla_r"),
             _w_in_rows(w_in_t, l, "ml_qk"), _w_in_rows(w_in_t, l, "ml_v"), _w_in_rows(w_in_t, l, "ml_o"),
             _w_in_rows(w_in_t, l, "ret"), _layer(w_small, l), _layer(wg_gla, l),
             gla_b_gate[l][None], b_gate[l][None]],
            [D_MODEL, U_SB, U_GLA, U_ML, U_RET], [BF16, BF16, F32, F32, F32])
        y_sb = _sb_call(u_sb, c["nuo"], batch, lt)
        o_gla, o_ml, o_ret = _recurrent_call(
            batch, lt, [u_gla, u_ml, u_ret],
            [ml_conv[l], c["tril"], c["e3"], c["bd_gla"], c["bd"], c["bdones"], c["onestack"],
             c["ret_dec"], c["ret_qdec"], c["ret_kdec"], c["ret_gam"]])
        (h,) = _row_call(
            _merge_kernel, "merge", batch, lt,
            [_full(hn), _full(h), _full(y_sb), _full(o_gla), _full(o_ml), _full(o_ret),
             (u_gla, BW, 2), (u_ml, BW, 3), (u_ret, BW, 3)], [],
            [_w_in_rows(w_in_t, l, "merge"), b_merge[l].reshape(1, N_BRANCH * D_MODEL), _layer(wb, l, True),
             _layer(wo, l, True),
             norm_mix_post[l][None],
             gla_norm[l][None], ml_norm[l][None], ret_norm[l][None], c["bdmean"]],
            [D_MODEL], [F32], acc_width=D_MODEL)
        ffn_consts = [norm_ffn_pre[l][None], _layer(wfg, l, True), _layer(wfu, l, True), _layer(wfd, l, True),
                      norm_ffn_post[l][None]]
        if l + 1 < depth:
            h = _ffn_call(h, batch, lt, ffn_consts).reshape(batch * lt, d)
        else:
            return _ffn_call(h, batch, lt, ffn_consts, final_seq=seq)
```

```python
import functools

import jax
import jax.numpy as jnp
from jax import lax
from jax.experimental import pallas as pl
from jax.experimental.pallas import tpu as pltpu

F32 = jnp.float32
BF16 = jnp.bfloat16

D_MODEL = 1024
BLOCK = 128
HEAD_DIM = 64
N_HEADS = 4
BW = N_HEADS * HEAD_DIM
N_BRANCH = 4
N_META = 16
N_FRONT_PAD = BLOCK - N_META
GLA_DK = 32
GLA_LOW_RANK = 16
GLA_GATE_NORMALIZER = 16.0
CONV_WIDTH = 4
D_FF = 2816
FF_CHUNK = 256
MERGE_CHUNK = 256
ROPE_BASE = 10000.0
NORM_EPS = 1e-6
SB_LOG_ZERO = -104.0
SB_STATIC_TILES = 3
V7X_VMEM_LIMIT = 56 * 1024 * 1024

CODE_LANE = 0
I_LANE = GLA_LOW_RANK
F_LANE = GLA_LOW_RANK + N_HEADS

_C = {}
_off = 0
for _name, _w in (("sb", 768), ("gla_q", 128), ("gla_k", 128), ("gla_v", 256), ("gla_r", 256), ("gla_code", 16),
                  ("ml_qk", 512), ("ml_v", 256), ("ml_i", 4), ("ml_f", 4), ("ml_o", 256), ("ret", 1024),
                  ("merge", 4096)):
    _C[_name] = (_off, _off + _w)
    _off += _w
IN_WIDTH = _off

U_SB, U_GLA, U_ML, U_RET = 768, 896, 1152, 1024
PROJ_FEATURES = 768 + 768 + 512 + 256 + 256 + 1024 + BLOCK


def _mm(a, b):
    return jnp.dot(a, b, preferred_element_type=F32)


def _mm_nt(a, b):
    return lax.dot_general(a, b, (((1,), (1,)), ((), ())), preferred_element_type=F32)


def _split(x):
    hi = x.astype(BF16)
    lo = (x - hi.astype(F32)).astype(BF16)
    return hi, lo


def _mm_split_r(c2, x):
    hi, lo = _split(x)
    return _mm(c2, jnp.concatenate([hi, lo], axis=0))


def _softplus(x):
    return jnp.maximum(x, 0.0) + jnp.log(1.0 + jnp.exp(-jnp.abs(x)))


def _log_sigmoid(x):
    return -_softplus(-x)


def _sigmoid(x):
    return 1.0 / (1.0 + jnp.exp(-x))


def _rmsnorm(x, g):
    return x * lax.rsqrt(jnp.mean(x * x, axis=-1, keepdims=True) + NORM_EPS) * g


def _head_rmsnorm(y, g, bdmean):
    ms = _mm((y * y).astype(BF16), bdmean)
    return y * lax.rsqrt(ms + NORM_EPS) * g


def _lanes_in(width, lo, hi):
    lane = lax.broadcasted_iota(jnp.int32, (1, width), 1)
    return (lane >= lo) & (lane < hi)


def _row_stack_heads(x, per_head, n=N_HEADS):
    keep = [_lanes_in(x.shape[1], per_head * h, per_head * (h + 1)).astype(x.dtype) for h in range(n)]
    return jnp.concatenate([x * m for m in keep], axis=0)


def _pair_scores(q_b, k_b):
    zs = []
    for p in range(N_HEADS // 2):
        lanes = slice(BLOCK * p, BLOCK * (p + 1))
        zs.append(_mm_nt(q_b[:, lanes], _row_stack_heads(k_b[:, lanes], HEAD_DIM, 2)))
    return jnp.concatenate(zs, axis=1)


def _pair_apply(s_b, v_b, extra=None):
    outs, extras = [], []
    for p in range(N_HEADS // 2):
        rhs = _row_stack_heads(v_b[:, BLOCK * p:BLOCK * (p + 1)], HEAD_DIM, 2)
        if extra is not None:
            rhs = jnp.concatenate([rhs, extra], axis=1)
        r = _mm(s_b[:, 2 * BLOCK * p:2 * BLOCK * (p + 1)], rhs)
        outs.append(r[:, :BLOCK])
        extras.append(r[:, BLOCK:])
    out = jnp.concatenate(outs, axis=1)
    return out if extra is None else (out, jnp.concatenate(extras, axis=1))


def _tile4(x):
    return jnp.concatenate([x] * N_HEADS, axis=1)


def _rotary(x, cos, sin_signed):
    first_half = _lanes_in(BLOCK, 0, 32) | _lanes_in(BLOCK, 64, 96)
    outs = []
    for t in range(2):
        lanes = slice(BLOCK * t, BLOCK * (t + 1))
        xt = x[:, lanes]
        swapped = jnp.where(first_half, pltpu.roll(xt, 96, 1), pltpu.roll(xt, 32, 1))
        outs.append(xt * cos[:, lanes] + swapped * sin_signed[:, lanes])
    return jnp.concatenate(outs, axis=1)


def _proj_kernel(h_ref, tail_ref, cos_ref, sin_ref, g_ref, wsb_ref, wgla_ref, wmlqk_ref, wmlv_ref, wmlo_ref, wret_ref,
                 wsmall_ref, wg_ref, bg_ref, bgate_ref, cw_ref, hn_ref, usb_ref, ugla_ref, uml_ref, uret_ref,
                 xbuf_ref, wbf_ref, *, tm, seq):
    j = pl.program_id(1)
    lt = tm * pl.num_programs(1)
    g = g_ref[...]

    def normed(x, first_pos):
        pos = first_pos + lax.broadcasted_iota(jnp.int32, (x.shape[0], 1), 0)
        valid = ((pos < seq) | (pos >= seq + N_FRONT_PAD)).astype(F32)
        return (_rmsnorm(x, g) * valid).astype(BF16), pos

    groups = {"sb": wsb_ref, "gla": wgla_ref, "ml_qk": wmlqk_ref, "ml_v": wmlv_ref, "ml_o": wmlo_ref,
              "ret": wret_ref, "small": wsmall_ref}
    rows_of, offset = {}, 0
    for name, w_ref in groups.items():
        n = w_ref.shape[-2]
        rows_of[name] = slice(offset, offset + n)
        offset += n

    @pl.when(jnp.logical_and(pl.program_id(0) == 0, j == 0))
    def _():
        for name, w_ref in groups.items():
            wbf_ref[rows_of[name], :] = (w_ref[0] if len(w_ref.shape) == 3 else w_ref[...]).astype(BF16)

    hn, pos = normed(h_ref[...], j * tm)
    hn_ref[...] = hn
    proj = lambda name: _mm_nt(hn, wbf_ref[rows_of[name], :])
    tail_hn, _ = normed(tail_ref[...], jnp.where(j == 0, lt - 8, j * tm - 8))
    xbuf_ref[0:8, :] = _mm_nt(tail_hn, wbf_ref[rows_of["ml_qk"], :])
    xbuf_ref[8:8 + tm, :] = proj("ml_qk")
    small = proj("small")
    ret = proj("ret")
    log_a_pre = _mm(small.astype(BF16), wg_ref[...])

    def conv_silu(lanes, scale):
        conv = cw_ref[CONV_WIDTH - 1:CONV_WIDTH, lanes] * xbuf_ref[8:8 + tm, lanes]
        for back in range(1, CONV_WIDTH):
            shifted = xbuf_ref[8 - back:8 - back + tm, lanes]
            before_start = (pos >= seq) & (pos < seq + back)
            conv = conv + (cw_ref[CONV_WIDTH - 1 - back:CONV_WIDTH - back, lanes]
                           * jnp.where(before_start, 0.0, shifted))
        uml_ref[:, lanes] = conv * _sigmoid(conv) * scale

    k_scale = HEAD_DIM ** -0.5
    usb_ref[...] = proj("sb").astype(BF16)
    conv_silu(slice(0, BLOCK), 1.0)
    cos = cos_ref[...]
    sin = sin_ref[...]
    uret_ref[:, 0:256] = _rotary(ret[:, 0:256], cos, sin)
    uret_ref[:, 256:512] = _rotary(ret[:, 256:512], cos, sin) * (HEAD_DIM ** -0.5)
    uret_ref[:, 512:1024] = ret[:, 512:1024]
    ugla_ref[:, 0:768] = proj("gla")
    conv_silu(slice(BLOCK, 2 * BLOCK), 1.0)
    ugla_ref[:, 768:896] = _log_sigmoid(log_a_pre + bg_ref[...]) * (1.0 / GLA_GATE_NORMALIZER)
    gates = small + bgate_ref[...]
    uml_ref[:, 1024:1152] = jnp.where(_lanes_in(BLOCK, F_LANE, F_LANE + N_HEADS), _log_sigmoid(gates), gates)
    uml_ref[:, 512:768] = proj("ml_v")
    conv_silu(slice(2 * BLOCK, 3 * BLOCK), k_scale)
    uml_ref[:, 768:1024] = proj("ml_o")
    conv_silu(slice(3 * BLOCK, 4 * BLOCK), k_scale)


def _sb_tile(q, k_ref, v_ref, j, n_blocks, bias, nuo2, cb):
    state = {"cb": cb, "out": None}
    for _ in _sb_tile_stages(q, k_ref, v_ref, j, n_blocks, bias, nuo2, state):
        pass
    return state["out"], state["cb"]


def _sb_tile_stages(q, k_ref, v_ref, j, n_blocks, bias, nuo2, state):
    start = pl.multiple_of(jnp.where(j <= 0, n_blocks - 1, j - 1) * BLOCK, BLOCK)
    kj = k_ref[pl.ds(start, BLOCK), :]
    zs = []
    for p in range(N_HEADS // 2):
        lanes = slice(BLOCK * p, BLOCK * (p + 1))
        k_rows = _row_stack_heads(kj[:, lanes], HEAD_DIM, 2)
        zs.append(_mm_nt(q[:, lanes], k_rows))
    yield
    z = jnp.concatenate(zs, axis=1) + bias
    hi, lo = _split(_softplus(z))
    rs = []
    for h in range(N_HEADS):
        lanes = slice(BLOCK * h, BLOCK * (h + 1))
        rs.append(_mm(jnp.concatenate([hi[:, lanes], lo[:, lanes]], axis=1), nuo2))
    yield
    cb = state["cb"]
    a = jnp.exp(z + jnp.concatenate([r[:, :BLOCK] for r in rs], axis=1) + cb).astype(BF16)
    state["cb"] = cb + jnp.concatenate([r[:, BLOCK:] for r in rs], axis=1)
    contrib = _mm(a, _row_stack_heads(v_ref[pl.ds(start, BLOCK), :], HEAD_DIM))
    yield
    acc_ref = state.get("acc_ref")
    if acc_ref is None:
        state["out"] = contrib if state["out"] is None else state["out"] + contrib
    elif state["out"] is None:
        acc_ref[...] = contrib
        state["out"] = acc_ref
    else:
        acc_ref[...] += contrib
    yield


def _sb_kernel(q_ref, k_ref, v_ref, nuo_ref, o_ref, acc_ref, cb_ref):
    i = pl.program_id(0)
    n_blocks = pl.num_programs(0)
    batch = q_ref.shape[0]
    qs = [q_ref[b] * jnp.asarray(HEAD_DIM ** -0.5, BF16) for b in range(batch)]
    row = lax.broadcasted_iota(jnp.int32, (BLOCK, BLOCK), 0)
    col = lax.broadcasted_iota(jnp.int32, (BLOCK, BLOCK), 1)
    nuo2 = nuo_ref[...]

    def bias_for(j, causal):
        ok = (j * BLOCK + col) >= N_FRONT_PAD
        if causal:
            ok = ok & (col < row)
        return jnp.concatenate([jnp.where(ok, 0.0, -jnp.inf)] * N_HEADS, axis=1)

    def all_max(x):
        return jnp.max(jnp.max(x, axis=1, keepdims=True), axis=0, keepdims=True)[0, 0]

    biases = [bias_for(i - t, t == 0) for t in range(SB_STATIC_TILES)]
    states = [{"cb": jnp.zeros((BLOCK, N_HEADS * BLOCK), F32), "out": None, "acc_ref": acc_ref.at[b]}
              for b in range(batch)]
    tiles = [_sb_tile_stages(qs[b], k_ref.at[b], v_ref.at[b], i - t, n_blocks, biases[t], nuo2, states[b])
             for t in range(SB_STATIC_TILES) for b in range(batch)]
    for _ in range(4):
        for tile in tiles:
            next(tile)

    def cond(c):
        j, mx = c
        return jnp.logical_and(j >= 0, mx > SB_LOG_ZERO)

    for b in range(batch):
        cb_ref[b] = states[b]["cb"]

        def body(c, b=b):
            j, _ = c
            contrib, cbn = _sb_tile(qs[b], k_ref.at[b], v_ref.at[b], j, n_blocks, bias_for(j, False), nuo2,
                                    cb_ref[b])
            acc_ref[b] += contrib
            cb_ref[b] = cbn
            return j - 1, all_max(cbn)

        lax.while_loop(cond, body, (i - SB_STATIC_TILES, all_max(states[b]["cb"])))
        o_ref[b] = acc_ref[b]


def _gla_chunk(u_ref, tril_ref, bd_ref, o_ref, st_ref):
    q = u_ref[:, 0:128] * (GLA_DK ** -0.5)
    k = u_ref[:, 128:256]
    v = u_ref[:, 256:512]
    log_a = u_ref[:, 768:896]
    cum = _mm_split_r(tril_ref[...], log_a)
    yield
    cum_last = cum[BLOCK - 1:BLOCK, :]
    qd = (q * jnp.exp(cum)).astype(BF16)
    kd = (k * jnp.exp(-cum)).astype(BF16)
    ke = (k * jnp.exp(cum_last - cum)).astype(BF16)
    st = st_ref[...]
    inter = _mm_nt(qd, st.astype(BF16))
    scores = _mm_nt(qd, _row_stack_heads(kd, GLA_DK))
    local = _mm(v.T.astype(BF16), ke)
    yield
    row = lax.broadcasted_iota(jnp.int32, (BLOCK, BLOCK), 0)
    col = lax.broadcasted_iota(jnp.int32, (BLOCK, BLOCK), 1)
    s = jnp.where(_tile4(row >= col), scores, 0.0)
    intra = _pair_apply(s.astype(BF16), v.astype(BF16))
    yield
    st_ref[...] = st * jnp.exp(cum_last) + local * bd_ref[...]
    o_ref[...] = intra + inter


def _ret_chunk(u_ref, dec_ref, qdec_ref, kdec_ref, gam_ref, bd_ref, o_ref, st_ref):
    qr = u_ref[:, 0:256]
    kr = u_ref[:, 256:512]
    v = u_ref[:, 512:768]
    scores = _pair_scores(qr.astype(BF16), kr.astype(BF16))
    st = st_ref[...]
    inter = _mm_nt((qr * qdec_ref[...]).astype(BF16), st.astype(BF16))
    local = _mm(v.T.astype(BF16), (kr * kdec_ref[...]).astype(BF16))
    yield
    s = scores * dec_ref[...]
    intra = _pair_apply(s.astype(BF16), v.astype(BF16))
    yield
    st_ref[...] = st * gam_ref[...] + local * bd_ref[...]
    o_ref[...] = intra + inter


def _spread_head(x_t, h):
    own = _lanes_in(BLOCK, 0, HEAD_DIM) if h % 2 == 0 else _lanes_in(BLOCK, HEAD_DIM, BLOCK)
    return jnp.where(own, x_t, pltpu.roll(x_t, HEAD_DIM, 1))


def _ml_chunk(u_ref, tril_ref, e3_ref, bd_ref, bdones_ref, onestack_ref, o_ref, ct_ref, n_ref, m_ref):
    gates = u_ref[:, 1024:1152]
    gates_cum = _mm_split_r(tril_ref[...], gates)
    yield
    compact = jnp.where(_lanes_in(BLOCK, F_LANE, F_LANE + N_HEADS), gates_cum, gates)
    t1 = compact.astype(BF16)
    rest = compact - t1.astype(F32)
    t2 = rest.astype(BF16)
    t3 = (rest - t2.astype(F32)).astype(BF16)
    spread = _mm(jnp.concatenate([t1, t2, t3], axis=1), e3_ref[...])
    q = u_ref[:, 0:BW]
    k = u_ref[:, BW:2 * BW]
    v = u_ref[:, 512:768]
    q_b = q.astype(BF16)
    k_b = k.astype(BF16)
    qk_all = _pair_scores(q_b, k_b)
    m_s = m_ref[0:1, :]
    n_s = n_ref[0:1, :]
    ct = ct_ref[...]
    q_ct = _mm_nt(q_b, ct.astype(BF16))
    q_n = _mm((q * n_s).astype(BF16), bdones_ref[...])
    yield
    log_i = spread[:, 0:BW]
    cum = spread[:, BW:2 * BW]

    cum_last = cum[BLOCK - 1:BLOCK, :]
    w_end = cum_last - cum + log_i
    a_end = jnp.max(w_end, axis=0, keepdims=True)
    kp = k * jnp.exp(w_end - a_end)
    v_b = v.astype(BF16)
    c_loc = _mm(v.T.astype(BF16), kp.astype(BF16)) * bd_ref[...]
    n_loc = jnp.sum(kp, axis=0, keepdims=True)

    inter_log = cum + m_s
    d_t = (log_i - cum).T
    yield
    row = lax.broadcasted_iota(jnp.int32, (BLOCK, BLOCK), 0)
    col = lax.broadcasted_iota(jnp.int32, (BLOCK, BLOCK), 1)
    s_parts, m_parts = [], []
    for h in range(N_HEADS):
        if h == N_HEADS // 2:
            yield
        t = h // 2
        cq = _spread_head(cum[:, BLOCK * t:BLOCK * (t + 1)], h)
        ilq = _spread_head(inter_log[:, BLOCK * t:BLOCK * (t + 1)], h)
        dk = jnp.broadcast_to(d_t[HEAD_DIM * h:HEAD_DIM * h + 1, :], (BLOCK, BLOCK))
        intra_log = jnp.where(row >= col, cq + dk, -jnp.inf)
        m_h = jnp.maximum(ilq, jnp.max(intra_log, axis=1, keepdims=True))
        s_parts.append(qk_all[:, BLOCK * h:BLOCK * (h + 1)] * jnp.exp(intra_log - m_h))
        m_parts.append(m_h)
    first = _lanes_in(BLOCK, 0, HEAD_DIM)
    m_t = jnp.concatenate([jnp.where(first, m_parts[0], m_parts[1]),
                           jnp.where(first, m_parts[2], m_parts[3])], axis=1)
    w_inter = jnp.exp(inter_log - m_t)
    s_b = jnp.concatenate(s_parts, axis=1).astype(BF16)
    num_intra, den_intra = _pair_apply(s_b, v_b, extra=onestack_ref[...])
    yield
    m_new = jnp.maximum(cum_last + m_s, a_end)
    s_prev = jnp.exp(cum_last + m_s - m_new)
    s_loc = jnp.exp(a_end - m_new)
    ct_ref[...] = s_prev * ct + s_loc * c_loc
    n_ref[...] = jnp.broadcast_to(s_prev * n_s + s_loc * n_loc, n_ref.shape)
    m_ref[...] = jnp.broadcast_to(m_new, m_ref.shape)
    num = num_intra + w_inter * q_ct
    den = den_intra + w_inter * q_n
    o_ref[...] = num / jnp.maximum(jnp.abs(den), jnp.exp(-m_t))


def _recurrent_kernel(ugla_ref, uml_ref, uret_ref, tril_ref, e3_ref, bdgla_ref, bd_ref, bdones_ref,
                      onestack_ref, dec_ref, qdec_ref, kdec_ref, gam_ref,
                      ogla_ref, oml_ref, oret_ref,
                      stgla_ref, ct_ref, n_ref, m_ref, stret_ref):
    @pl.when(pl.program_id(0) == 0)
    def _():
        for ref in (stgla_ref, ct_ref, n_ref, m_ref, stret_ref):
            ref[...] = jnp.zeros_like(ref)

    chains = []
    for b in range(ugla_ref.shape[0]):
        chains += [
            _ml_chunk(uml_ref.at[b], tril_ref, e3_ref, bd_ref, bdones_ref, onestack_ref,
                      oml_ref.at[b], ct_ref.at[b], n_ref.at[b], m_ref.at[b]),
            _gla_chunk(ugla_ref.at[b], tril_ref, bdgla_ref, ogla_ref.at[b], stgla_ref.at[b]),
            _ret_chunk(uret_ref.at[b], dec_ref, qdec_ref, kdec_ref, gam_ref, bd_ref, oret_ref.at[b],
                       stret_ref.at[b])]
    while chains:
        chains = [chain for chain in chains if next(chain, True) is None]


def _merge_kernel(hn_ref, h_ref, ysb_ref, ogla_ref, oml_ref, oret_ref, r_ref, opre_ref, gret_ref,
                  wm_ref, bm_ref, wb_ref, wo_ref, g_ref, ngla_ref, nml_ref, nret_ref, bdmean_ref, o_ref, acc_ref):
    hn = hn_ref[...]
    bdmean = bdmean_ref[...]
    r = r_ref[...]
    g = gret_ref[...]
    branches = [y.astype(BF16) for y in (
        ysb_ref[...],
        _head_rmsnorm(ogla_ref[...], ngla_ref[...], bdmean) * (r * _sigmoid(r)),
        _sigmoid(opre_ref[...]) * _head_rmsnorm(oml_ref[...], nml_ref[...], bdmean),
        (g * _sigmoid(g)) * _head_rmsnorm(oret_ref[...], nret_ref[...], bdmean))]
    for c in range(D_MODEL // MERGE_CHUNK):
        out_cols = slice(MERGE_CHUNK * c, MERGE_CHUNK * (c + 1))
        merged = None
        for n, y_b in enumerate(branches):
            cols = slice(D_MODEL * n + MERGE_CHUNK * c, D_MODEL * n + MERGE_CHUNK * (c + 1))
            gate = _sigmoid(_mm_nt(hn, wm_ref[0, cols, :].astype(BF16)) + bm_ref[:, cols])
            term = gate * _mm(y_b, wb_ref[n, :, out_cols].astype(BF16))
            merged = term if merged is None else merged + term
        part = _mm(merged.astype(BF16), wo_ref[out_cols, :].astype(BF16))
        if c == 0:
            acc_ref[...] = part
        else:
            acc_ref[...] += part
    o_ref[...] = h_ref[...] + _rmsnorm(acc_ref[...], g_ref[...])


def _ffn_kernel(h_ref, gpre_ref, wg_ref, wu_ref, wd_ref, gpost_ref, o_ref, acc_ref):
    x = h_ref[...]
    f = _rmsnorm(x, gpre_ref[...]).astype(BF16)
    for c in range(D_FF // FF_CHUNK):
        cols = slice(FF_CHUNK * c, FF_CHUNK * (c + 1))
        a = _mm(f, wg_ref[:, cols].astype(BF16))
        act = ((a * _sigmoid(a)) * _mm(f, wu_ref[:, cols].astype(BF16))).astype(BF16)
        part = _mm(act, wd_ref[cols, :].astype(BF16))
        if c == 0:
            acc_ref[...] = part
        else:
            acc_ref[...] += part
    o_ref[...] = x + _rmsnorm(acc_ref[...], gpost_ref[...])


def _largest_tile(n, candidates):
    for tm in candidates:
        if n % tm == 0:
            return tm
    raise ValueError(f"{n} rows are not a multiple of {BLOCK}")


def _const_spec(shape, grid_rank):
    zeros = (0,) * len(shape)
    return pl.BlockSpec(shape, (lambda b, j: zeros) if grid_rank == 2 else (lambda n: zeros))


def _params(sem):
    return pltpu.CompilerParams(dimension_semantics=sem, vmem_limit_bytes=V7X_VMEM_LIMIT)


def _row_call(kernel, name, batch, lt, row_inputs, tile_inputs, const_inputs, out_widths, out_dtypes,
              scratch=None):
    tm = _largest_tile(lt, (640, 512, 384, 256, 128))
    nb = lt // tm
    rows = batch * lt
    row_spec = lambda w, cb: pl.BlockSpec((tm, w), lambda b, j: (b * nb + j, cb))
    return pl.pallas_call(
        kernel,
        grid=(batch, nb),
        in_specs=([a[1] if len(a) == 2 else row_spec(a[1], a[2]) for a in row_inputs]
                  + [pl.BlockSpec((tm, a.shape[1]), lambda b, j: (j, 0)) for a in tile_inputs]
                  + [a[1] if isinstance(a, tuple) else _const_spec(a.shape, 2) for a in const_inputs]),
        out_specs=[row_spec(w, 0) for w in out_widths],
        out_shape=[jax.ShapeDtypeStruct((rows, w), dt) for w, dt in zip(out_widths, out_dtypes)],
        scratch_shapes=[pltpu.VMEM(((tm if per_tile else 0) + n, w), dt) for per_tile, n, w, dt in (scratch or [])],
        compiler_params=_params(("arbitrary", "arbitrary")),
        name=name,
    )(*[a[0] for a in row_inputs], *tile_inputs, *[a[0] if isinstance(a, tuple) else a for a in const_inputs])


def _preceding_rows(h, batch, lt):
    tm = _largest_tile(lt, (640, 512, 384, 256, 128))
    per_batch, per_tile = lt // 8, tm // 8
    index = lambda b, j: (b * per_batch + jnp.where(j == 0, per_batch - 1, j * per_tile - 1), 0)
    return (h, pl.BlockSpec((8, h.shape[1]), index))


def _w_in_rows(w_in_t, layer, first, last=None):
    start, stop = _C[first][0], _C[last or first][1]
    block = (pl.Element(1), pl.Element(stop - start), pl.Element(D_MODEL))
    return (w_in_t, pl.BlockSpec(block, lambda b, j: (layer, start, 0), pipeline_mode=pl.Buffered(1)))


def _layer(stacked, layer, single_buffer=False):
    zeros = (0,) * (stacked.ndim - 1)
    mode = pl.Buffered(1) if single_buffer else None
    return (stacked, pl.BlockSpec((None,) + stacked.shape[1:], lambda b, j: (layer,) + zeros, pipeline_mode=mode))


def _full(a):
    return (a, a.shape[1], 0)


def _ffn_call(h, batch, lt, consts, final_seq=None):
    if final_seq is None:
        tm, rows_per_batch, out_rows = _largest_tile(lt, (640, 512, 384, 256, 128)), lt, lt
    else:
        tm, rows_per_batch, out_rows = _largest_tile(final_seq, (512, 256, 128)), final_seq, final_seq
    spec = pl.BlockSpec((None, tm, D_MODEL), lambda b, j: (b, j, 0))
    return pl.pallas_call(
        _ffn_kernel,
        grid=(batch, rows_per_batch // tm),
        in_specs=[spec] + [a[1] if isinstance(a, tuple) else _const_spec(a.shape, 2) for a in consts],
        out_specs=spec,
        out_shape=jax.ShapeDtypeStruct((batch, out_rows, D_MODEL), F32),
        scratch_shapes=[pltpu.VMEM((tm, D_MODEL), F32)],
        compiler_params=_params(("parallel", "parallel")),
        name="swiglu",
    )(h.reshape(batch, lt, D_MODEL), *[a[0] if isinstance(a, tuple) else a for a in consts])


def _logical_to_memory_block(n, nc):
    return jnp.where(n == 0, nc - 1, n - 1)


def _recurrent_call(batch, lt, u_list, const_inputs):
    nc = lt // BLOCK
    u3 = [u.reshape(batch, lt, u.shape[1]) for u in u_list]
    chunk_map = lambda n: (0, _logical_to_memory_block(n, nc), 0)
    o_spec = pl.BlockSpec((batch, BLOCK, BW), chunk_map)
    outs = pl.pallas_call(
        _recurrent_kernel,
        grid=(nc,),
        in_specs=([pl.BlockSpec((batch, BLOCK, u.shape[2]), chunk_map) for u in u3]
                  + [_const_spec(a.shape, 1) for a in const_inputs]),
        out_specs=[o_spec] * 3,
        out_shape=[jax.ShapeDtypeStruct((batch, lt, BW), F32)] * 3,
        scratch_shapes=[pltpu.VMEM((batch, BW, BLOCK), F32),
                        pltpu.VMEM((batch, BW, BW), F32),
                        pltpu.VMEM((batch, 8, BW), F32),
                        pltpu.VMEM((batch, 8, BW), F32),
                        pltpu.VMEM((batch, BW, BW), F32)],
        compiler_params=_params(("arbitrary",)),
        name="recurrent_mixers",
    )(*u3, *const_inputs)
    return [o.reshape(batch * lt, BW) for o in outs]


def _sb_call(u_sb, nuo, batch, lt):
    nc = lt // BLOCK
    u3 = u_sb.reshape(batch, lt, U_SB)
    q_spec = pl.BlockSpec((batch, BLOCK, BW), lambda i: (0, _logical_to_memory_block(i, nc), 0))
    return pl.pallas_call(
        _sb_kernel,
        grid=(nc,),
        in_specs=[q_spec,
                  pl.BlockSpec((batch, lt, BW), lambda i: (0, 0, 1)),
                  pl.BlockSpec((batch, lt, BW), lambda i: (0, 0, 2)),
                  _const_spec(nuo.shape, 1)],
        out_specs=q_spec,
        out_shape=jax.ShapeDtypeStruct((batch, lt, BW), F32),
        scratch_shapes=[pltpu.VMEM((batch, BLOCK, BW), F32), pltpu.VMEM((batch, BLOCK, N_HEADS * BLOCK), F32)],
        compiler_params=_params(("parallel",)),
        name="stick_breaking",
    )(u3, u3, u3, nuo).reshape(batch * lt, BW)


def _constants(seq):
    idx = jnp.arange(BLOCK)
    tril = (idx[:, None] >= idx[None, :])
    nuo = -jnp.concatenate([tril, jnp.ones((BLOCK, BLOCK), bool)], axis=1).astype(BF16)
    c = {"tril": jnp.concatenate([tril, tril], axis=1).astype(BF16),
         "nuo": jnp.concatenate([nuo, nuo], axis=0)}
    head_e = jnp.arange(BW) // HEAD_DIM
    c["bd_gla"] = (head_e[:, None] == (jnp.arange(BLOCK) // GLA_DK)[None, :]).astype(F32)
    bd = head_e[:, None] == head_e[None, :]
    c["bd"] = bd.astype(F32)
    c["bdones"] = bd.astype(BF16)
    c["bdmean"] = (bd.astype(F32) / HEAD_DIM).astype(BF16)
    c["onestack"] = ((jnp.arange(2 * BLOCK) // BLOCK)[:, None]
                     == (jnp.arange(BLOCK) // HEAD_DIM)[None, :]).astype(BF16)
    lane = jnp.arange(BLOCK)[:, None]
    spread = jnp.concatenate([lane == I_LANE + head_e[None, :], lane == F_LANE + head_e[None, :]], axis=1)
    c["e3"] = jnp.concatenate([spread] * 3, axis=0).astype(BF16)
    positions = jnp.concatenate([jnp.arange(N_META, seq + N_META), jnp.arange(-N_FRONT_PAD, N_META)])
    half = HEAD_DIM // 2
    inv_freq = ROPE_BASE ** (-jnp.arange(half, dtype=F32) / half)
    lane_freq = jnp.tile(inv_freq, 2 * N_HEADS)
    lane_sign = jnp.tile(jnp.concatenate([-jnp.ones(half, F32), jnp.ones(half, F32)]), N_HEADS)
    ang = positions.astype(jnp.int32).astype(F32)[:, None] * lane_freq[None, :]
    c["cos"] = jnp.cos(ang)
    c["sin"] = jnp.sin(ang) * lane_sign[None, :]
    log_gamma = jnp.log1p(-jnp.exp2(-5.0 - jnp.arange(N_HEADS, dtype=F32)))
    pos = jnp.arange(BLOCK, dtype=F32)
    diff = jnp.maximum(pos[:, None] - pos[None, :], 0.0)
    dec = jnp.where(tril, jnp.exp(log_gamma[:, None, None] * diff), 0.0)
    c["ret_dec"] = jnp.moveaxis(dec, 0, 1).reshape(BLOCK, N_HEADS * BLOCK)
    c["ret_qdec"] = jnp.repeat(jnp.exp(log_gamma[:, None] * (pos + 1.0)).T, HEAD_DIM, axis=1)
    c["ret_kdec"] = jnp.repeat(jnp.exp(log_gamma[:, None] * (BLOCK - 1.0 - pos)).T, HEAD_DIM, axis=1)
    c["ret_gam"] = jnp.repeat(jnp.exp(log_gamma * BLOCK), HEAD_DIM)[None, :]
    return c


def _small_group(w_in_t):
    rows = lambda name: w_in_t[:, _C[name][0]:_C[name][1], :]
    pad = jnp.zeros((w_in_t.shape[0], BLOCK - GLA_LOW_RANK - 2 * N_HEADS, D_MODEL), w_in_t.dtype)
    return jnp.concatenate([rows("gla_code"), rows("ml_i"), rows("ml_f"), pad], axis=1)


def kernel(x, meta_tokens, norm_mix_pre, norm_mix_post, norm_ffn_pre, norm_ffn_post, w_in,
           gla_w_gate_up, gla_b_gate, gla_norm, ml_conv, ml_b_i, ml_b_f, ml_norm, ret_norm,
           w_branch, b_merge, w_out, ffn_w_gate, ffn_w_up, ffn_w_down):
    batch, seq, d = x.shape
    depth = w_in.shape[0]
    lt = seq + BLOCK
    pad = jnp.zeros((batch, N_FRONT_PAD, d), x.dtype)
    meta = jnp.broadcast_to(meta_tokens.astype(x.dtype)[None], (batch, N_META, d))
    h = jnp.concatenate([x, pad, meta], axis=1).reshape(batch * lt, d)

    c = _constants(seq)
    w_in_t = jnp.swapaxes(w_in, 1, 2)
    w_small = _small_group(w_in_t)
    wg_gla = jnp.pad(gla_w_gate_up, ((0, 0), (0, BLOCK - GLA_LOW_RANK), (0, 0))).astype(BF16)
    tail = BLOCK - GLA_LOW_RANK - 2 * N_HEADS
    b_gate = jnp.pad(jnp.concatenate([ml_b_i, ml_b_f], axis=-1), ((0, 0), (GLA_LOW_RANK, tail)))
    wb, wo = w_branch, w_out
    wfg, wfu, wfd = ffn_w_gate, ffn_w_up, ffn_w_down
    tm = _largest_tile(lt, (640, 512, 384, 256, 128))

    for l in range(depth):
        hn, u_sb, u_gla, u_ml, u_ret = _row_call(
            functools.partial(_proj_kernel, tm=tm, seq=seq), "in_projection", batch, lt,
            [_full(h), _preceding_rows(h, batch, lt)], [c["cos"], c["sin"]],
            [norm_mix_pre[l][None], _w_in_rows(w_in_t, l, "sb"), _w_in_rows(w_in_t, l, "gla_q", "gla_r"),
             _w_in_rows(w_in_t, l, "ml_qk"), _w_in_rows(w_in_t, l, "ml_v"), _w_in_rows(w_in_t, l, "ml_o"),
             _w_in_rows(w_in_t, l, "ret"), _layer(w_small, l), _layer(wg_gla, l),
             gla_b_gate[l][None], b_gate[l][None], ml_conv[l]],
            [D_MODEL, U_SB, U_GLA, U_ML, U_RET], [BF16, BF16, F32, F32, F32],
            scratch=[(True, 8, 2 * BW, F32), (False, PROJ_FEATURES, D_MODEL, BF16)])
        y_sb = _sb_call(u_sb, c["nuo"], batch, lt)
        o_gla, o_ml, o_ret = _recurrent_call(
            batch, lt, [u_gla, u_ml, u_ret],
            [c["tril"], c["e3"], c["bd_gla"], c["bd"], c["bdones"], c["onestack"],
             c["ret_dec"], c["ret_qdec"], c["ret_kdec"], c["ret_gam"]])
        (h,) = _row_call(
            _merge_kernel, "merge", batch, lt,
            [_full(hn), _full(h), _full(y_sb), _full(o_gla), _full(o_ml), _full(o_ret),
             (u_gla, BW, 2), (u_ml, BW, 3), (u_ret, BW, 3)], [],
            [_w_in_rows(w_in_t, l, "merge"), b_merge[l].reshape(1, N_BRANCH * D_MODEL), _layer(wb, l, True),
             _layer(wo, l, True),
             norm_mix_post[l][None],
             gla_norm[l][None], ml_norm[l][None], ret_norm[l][None], c["bdmean"]],
            [D_MODEL], [F32], scratch=[(True, 0, D_MODEL, F32)])
        ffn_consts = [norm_ffn_pre[l][None], _layer(wfg, l, True), _layer(wfu, l, True), _layer(wfd, l, True),
                      norm_ffn_post[l][None]]
        if l + 1 < depth:
            h = _ffn_call(h, batch, lt, ffn_consts).reshape(batch * lt, d)
        else:
            return _ffn_call(h, batch, lt, ffn_consts, final_seq=seq)
```

```python
import functools

import jax
import jax.numpy as jnp
from jax import lax
from jax.experimental import pallas as pl
from jax.experimental.pallas import tpu as pltpu

F32 = jnp.float32
BF16 = jnp.bfloat16

D_MODEL = 1024
BLOCK = 128
HEAD_DIM = 64
N_HEADS = 4
BW = N_HEADS * HEAD_DIM
N_BRANCH = 4
N_META = 16
N_FRONT_PAD = BLOCK - N_META
GLA_DK = 32
GLA_LOW_RANK = 16
GLA_GATE_NORMALIZER = 16.0
CONV_WIDTH = 4
D_FF = 2816
FF_CHUNK = 256
MERGE_CHUNK = 256
ROPE_BASE = 10000.0
NORM_EPS = 1e-6
SB_LOG_ZERO = -104.0
SB_STATIC_TILES = 3
V7X_VMEM_LIMIT = 56 * 1024 * 1024

CODE_LANE = 0
I_LANE = GLA_LOW_RANK
F_LANE = GLA_LOW_RANK + N_HEADS

_C = {}
_off = 0
for _name, _w in (("sb", 768), ("gla_q", 128), ("gla_k", 128), ("gla_v", 256), ("gla_r", 256), ("gla_code", 16),
                  ("ml_qk", 512), ("ml_v", 256), ("ml_i", 4), ("ml_f", 4), ("ml_o", 256), ("ret", 1024),
                  ("merge", 4096)):
    _C[_name] = (_off, _off + _w)
    _off += _w
IN_WIDTH = _off

U_SB, U_GLA, U_ML, U_RET = 768, 896, 1152, 1024


def _mm(a, b):
    return jnp.dot(a, b, preferred_element_type=F32)


def _mm_nt(a, b):
    return lax.dot_general(a, b, (((1,), (1,)), ((), ())), preferred_element_type=F32)


def _split(x):
    hi = x.astype(BF16)
    lo = (x - hi.astype(F32)).astype(BF16)
    return hi, lo


def _mm_split_r(c2, x):
    hi, lo = _split(x)
    return _mm(c2, jnp.concatenate([hi, lo], axis=0))


def _softplus(x):
    return jnp.maximum(x, 0.0) + jnp.log(1.0 + jnp.exp(-jnp.abs(x)))


def _log_sigmoid(x):
    return -_softplus(-x)


def _sigmoid(x):
    return 1.0 / (1.0 + jnp.exp(-x))


def _rmsnorm(x, g):
    return x * lax.rsqrt(jnp.mean(x * x, axis=-1, keepdims=True) + NORM_EPS) * g


def _head_rmsnorm(y, g, bdmean):
    ms = _mm((y * y).astype(BF16), bdmean)
    return y * lax.rsqrt(ms + NORM_EPS) * g


def _lanes_in(width, lo, hi):
    lane = lax.broadcasted_iota(jnp.int32, (1, width), 1)
    return (lane >= lo) & (lane < hi)


def _row_stack_heads(x, per_head, n=N_HEADS):
    keep = [_lanes_in(x.shape[1], per_head * h, per_head * (h + 1)).astype(x.dtype) for h in range(n)]
    return jnp.concatenate([x * m for m in keep], axis=0)


def _pair_scores(q_b, k_b):
    zs = []
    for p in range(N_HEADS // 2):
        lanes = slice(BLOCK * p, BLOCK * (p + 1))
        zs.append(_mm_nt(q_b[:, lanes], _row_stack_heads(k_b[:, lanes], HEAD_DIM, 2)))
    return jnp.concatenate(zs, axis=1)


def _pair_apply(s_b, v_b, extra=None):
    outs, extras = [], []
    for p in range(N_HEADS // 2):
        rhs = _row_stack_heads(v_b[:, BLOCK * p:BLOCK * (p + 1)], HEAD_DIM, 2)
        if extra is not None:
            rhs = jnp.concatenate([rhs, extra], axis=1)
        r = _mm(s_b[:, 2 * BLOCK * p:2 * BLOCK * (p + 1)], rhs)
        outs.append(r[:, :BLOCK])
        extras.append(r[:, BLOCK:])
    out = jnp.concatenate(outs, axis=1)
    return out if extra is None else (out, jnp.concatenate(extras, axis=1))


def _tile4(x):
    return jnp.concatenate([x] * N_HEADS, axis=1)


def _rotary(x, cos, sin_signed):
    first_half = _lanes_in(BLOCK, 0, 32) | _lanes_in(BLOCK, 64, 96)
    outs = []
    for t in range(2):
        lanes = slice(BLOCK * t, BLOCK * (t + 1))
        xt = x[:, lanes]
        swapped = jnp.where(first_half, pltpu.roll(xt, 96, 1), pltpu.roll(xt, 32, 1))
        outs.append(xt * cos[:, lanes] + swapped * sin_signed[:, lanes])
    return jnp.concatenate(outs, axis=1)


def _proj_kernel(h_ref, cos_ref, sin_ref, g_ref, wsb_ref, wgla_ref, wmlqk_ref, wmlv_ref, wmlo_ref, wret_ref,
                 wsmall_ref, wg_ref, bg_ref, bgate_ref, hn_ref, usb_ref, ugla_ref, uml_ref, uret_ref, *, tm, seq):
    x = h_ref[...]
    pos = pl.program_id(1) * tm + lax.broadcasted_iota(jnp.int32, (tm, 1), 0)
    valid = ((pos < seq) | (pos >= seq + N_FRONT_PAD)).astype(F32)
    hn = (_rmsnorm(x, g_ref[...]) * valid).astype(BF16)
    hn_ref[...] = hn
    proj = lambda w_ref: _mm_nt(hn, w_ref[0].astype(BF16))
    small = _mm_nt(hn, wsmall_ref[...].astype(BF16))
    ret = proj(wret_ref)
    log_a_pre = _mm(small.astype(BF16), wg_ref[...])
    usb_ref[...] = proj(wsb_ref).astype(BF16)
    cos = cos_ref[...]
    sin = sin_ref[...]
    uret_ref[:, 0:256] = _rotary(ret[:, 0:256], cos, sin)
    uret_ref[:, 256:512] = _rotary(ret[:, 256:512], cos, sin) * (HEAD_DIM ** -0.5)
    uret_ref[:, 512:1024] = ret[:, 512:1024]
    ugla_ref[:, 768:896] = _log_sigmoid(log_a_pre + bg_ref[...]) * (1.0 / GLA_GATE_NORMALIZER)
    gates = small + bgate_ref[...]
    uml_ref[:, 1024:1152] = jnp.where(_lanes_in(BLOCK, F_LANE, F_LANE + N_HEADS), _log_sigmoid(gates), gates)
    ugla_ref[:, 0:768] = proj(wgla_ref)
    uml_ref[:, 0:512] = proj(wmlqk_ref)
    uml_ref[:, 512:768] = proj(wmlv_ref)
    uml_ref[:, 768:1024] = proj(wmlo_ref)


def _sb_tile(q, k_ref, v_ref, j, n_blocks, bias, nuo2, cb):
    state = {"cb": cb, "out": None}
    for _ in _sb_tile_stages(q, k_ref, v_ref, j, n_blocks, bias, nuo2, state):
        pass
    return state["out"], state["cb"]


def _sb_tile_stages(q, k_ref, v_ref, j, n_blocks, bias, nuo2, state):
    start = pl.multiple_of(jnp.where(j <= 0, n_blocks - 1, j - 1) * BLOCK, BLOCK)
    kj = k_ref[pl.ds(start, BLOCK), :]
    zs = []
    for p in range(N_HEADS // 2):
        lanes = slice(BLOCK * p, BLOCK * (p + 1))
        k_rows = _row_stack_heads(kj[:, lanes], HEAD_DIM, 2)
        zs.append(_mm_nt(q[:, lanes], k_rows))
    yield
    z = jnp.concatenate(zs, axis=1) + bias
    hi, lo = _split(_softplus(z))
    rs = []
    for h in range(N_HEADS):
        lanes = slice(BLOCK * h, BLOCK * (h + 1))
        rs.append(_mm(jnp.concatenate([hi[:, lanes], lo[:, lanes]], axis=1), nuo2))
    yield
    cb = state["cb"]
    a = jnp.exp(z + jnp.concatenate([r[:, :BLOCK] for r in rs], axis=1) + cb).astype(BF16)
    state["cb"] = cb + jnp.concatenate([r[:, BLOCK:] for r in rs], axis=1)
    contrib = _mm(a, _row_stack_heads(v_ref[pl.ds(start, BLOCK), :], HEAD_DIM))
    yield
    acc_ref = state.get("acc_ref")
    if acc_ref is None:
        state["out"] = contrib if state["out"] is None else state["out"] + contrib
    elif state["out"] is None:
        acc_ref[...] = contrib
        state["out"] = acc_ref
    else:
        acc_ref[...] += contrib
    yield


def _sb_kernel(q_ref, k_ref, v_ref, nuo_ref, o_ref, acc_ref, cb_ref):
    i = pl.program_id(0)
    n_blocks = pl.num_programs(0)
    batch = q_ref.shape[0]
    qs = [q_ref[b] * jnp.asarray(HEAD_DIM ** -0.5, BF16) for b in range(batch)]
    row = lax.broadcasted_iota(jnp.int32, (BLOCK, BLOCK), 0)
    col = lax.broadcasted_iota(jnp.int32, (BLOCK, BLOCK), 1)
    nuo2 = nuo_ref[...]

    def bias_for(j, causal):
        ok = (j * BLOCK + col) >= N_FRONT_PAD
        if causal:
            ok = ok & (col < row)
        return jnp.concatenate([jnp.where(ok, 0.0, -jnp.inf)] * N_HEADS, axis=1)

    def all_max(x):
        return jnp.max(jnp.max(x, axis=1, keepdims=True), axis=0, keepdims=True)[0, 0]

    biases = [bias_for(i - t, t == 0) for t in range(SB_STATIC_TILES)]
    states = [{"cb": jnp.zeros((BLOCK, N_HEADS * BLOCK), F32), "out": None, "acc_ref": acc_ref.at[b]}
              for b in range(batch)]
    tiles = [_sb_tile_stages(qs[b], k_ref.at[b], v_ref.at[b], i - t, n_blocks, biases[t], nuo2, states[b])
             for t in range(SB_STATIC_TILES) for b in range(batch)]
    for _ in range(4):
        for tile in tiles:
            next(tile)

    def cond(c):
        j, mx = c
        return jnp.logical_and(j >= 0, mx > SB_LOG_ZERO)

    for b in range(batch):
        cb_ref[b] = states[b]["cb"]

        def body(c, b=b):
            j, _ = c
            contrib, cbn = _sb_tile(qs[b], k_ref.at[b], v_ref.at[b], j, n_blocks, bias_for(j, False), nuo2,
                                    cb_ref[b])
            acc_ref[b] += contrib
            cb_ref[b] = cbn
            return j - 1, all_max(cbn)

        lax.while_loop(cond, body, (i - SB_STATIC_TILES, all_max(states[b]["cb"])))
        o_ref[b] = acc_ref[b]


def _gla_chunk(u_ref, tril_ref, bd_ref, o_ref, st_ref):
    q = u_ref[:, 0:128] * (GLA_DK ** -0.5)
    k = u_ref[:, 128:256]
    v = u_ref[:, 256:512]
    log_a = u_ref[:, 768:896]
    cum = _mm_split_r(tril_ref[...], log_a)
    yield
    cum_last = cum[BLOCK - 1:BLOCK, :]
    qd = (q * jnp.exp(cum)).astype(BF16)
    kd = (k * jnp.exp(-cum)).astype(BF16)
    ke = (k * jnp.exp(cum_last - cum)).astype(BF16)
    st = st_ref[...]
    inter = _mm_nt(qd, st.astype(BF16))
    scores = _mm_nt(qd, _row_stack_heads(kd, GLA_DK))
    local = _mm(v.T.astype(BF16), ke)
    yield
    row = lax.broadcasted_iota(jnp.int32, (BLOCK, BLOCK), 0)
    col = lax.broadcasted_iota(jnp.int32, (BLOCK, BLOCK), 1)
    s = jnp.where(_tile4(row >= col), scores, 0.0)
    intra = _pair_apply(s.astype(BF16), v.astype(BF16))
    yield
    st_ref[...] = st * jnp.exp(cum_last) + local * bd_ref[...]
    o_ref[...] = intra + inter


def _ret_chunk(u_ref, dec_ref, qdec_ref, kdec_ref, gam_ref, bd_ref, o_ref, st_ref):
    qr = u_ref[:, 0:256]
    kr = u_ref[:, 256:512]
    v = u_ref[:, 512:768]
    scores = _pair_scores(qr.astype(BF16), kr.astype(BF16))
    st = st_ref[...]
    inter = _mm_nt((qr * qdec_ref[...]).astype(BF16), st.astype(BF16))
    local = _mm(v.T.astype(BF16), (kr * kdec_ref[...]).astype(BF16))
    yield
    s = scores * dec_ref[...]
    intra = _pair_apply(s.astype(BF16), v.astype(BF16))
    yield
    st_ref[...] = st * gam_ref[...] + local * bd_ref[...]
    o_ref[...] = intra + inter


def _spread_head(x_t, h):
    own = _lanes_in(BLOCK, 0, HEAD_DIM) if h % 2 == 0 else _lanes_in(BLOCK, HEAD_DIM, BLOCK)
    return jnp.where(own, x_t, pltpu.roll(x_t, HEAD_DIM, 1))


def _ml_chunk(u_ref, cw_ref, tril_ref, e3_ref, bd_ref, bdones_ref, onestack_ref, o_ref, xbuf_ref, ct_ref, n_ref,
              m_ref):
    gates = u_ref[:, 1024:1152]
    gates_cum = _mm_split_r(tril_ref[...], gates)
    xbuf_ref[8:8 + BLOCK, :] = u_ref[:, 0:512]
    conv = None
    for j in range(CONV_WIDTH):
        term = cw_ref[j:j + 1, :] * xbuf_ref[8 - (CONV_WIDTH - 1) + j:8 - (CONV_WIDTH - 1) + j + BLOCK, :]
        conv = term if conv is None else conv + term
    xbuf_ref[0:8, :] = xbuf_ref[BLOCK:BLOCK + 8, :]
    yield
    compact = jnp.where(_lanes_in(BLOCK, F_LANE, F_LANE + N_HEADS), gates_cum, gates)
    t1 = compact.astype(BF16)
    rest = compact - t1.astype(F32)
    t2 = rest.astype(BF16)
    t3 = (rest - t2.astype(F32)).astype(BF16)
    spread = _mm(jnp.concatenate([t1, t2, t3], axis=1), e3_ref[...])
    qk = conv * _sigmoid(conv)
    q = qk[:, 0:BW]
    k = qk[:, BW:2 * BW] * (HEAD_DIM ** -0.5)
    v = u_ref[:, 512:768]
    q_b = q.astype(BF16)
    k_b = k.astype(BF16)
    qk_all = _pair_scores(q_b, k_b)
    m_s = m_ref[0:1, :]
    n_s = n_ref[0:1, :]
    ct = ct_ref[...]
    q_ct = _mm_nt(q_b, ct.astype(BF16))
    q_n = _mm((q * n_s).astype(BF16), bdones_ref[...])
    yield
    log_i = spread[:, 0:BW]
    cum = spread[:, BW:2 * BW]

    cum_last = cum[BLOCK - 1:BLOCK, :]
    w_end = cum_last - cum + log_i
    a_end = jnp.max(w_end, axis=0, keepdims=True)
    kp = k * jnp.exp(w_end - a_end)
    v_b = v.astype(BF16)
    c_loc = _mm(v.T.astype(BF16), kp.astype(BF16)) * bd_ref[...]
    n_loc = jnp.sum(kp, axis=0, keepdims=True)

    inter_log = cum + m_s
    d_t = (log_i - cum).T
    yield
    row = lax.broadcasted_iota(jnp.int32, (BLOCK, BLOCK), 0)
    col = lax.broadcasted_iota(jnp.int32, (BLOCK, BLOCK), 1)
    s_parts, m_parts = [], []
    for h in range(N_HEADS):
        if h == N_HEADS // 2:
            yield
        t = h // 2
        cq = _spread_head(cum[:, BLOCK * t:BLOCK * (t + 1)], h)
        ilq = _spread_head(inter_log[:, BLOCK * t:BLOCK * (t + 1)], h)
        dk = jnp.broadcast_to(d_t[HEAD_DIM * h:HEAD_DIM * h + 1, :], (BLOCK, BLOCK))
        intra_log = jnp.where(row >= col, cq + dk, -jnp.inf)
        m_h = jnp.maximum(ilq, jnp.max(intra_log, axis=1, keepdims=True))
        s_parts.append(qk_all[:, BLOCK * h:BLOCK * (h + 1)] * jnp.exp(intra_log - m_h))
        m_parts.append(m_h)
    first = _lanes_in(BLOCK, 0, HEAD_DIM)
    m_t = jnp.concatenate([jnp.where(first, m_parts[0], m_parts[1]),
                           jnp.where(first, m_parts[2], m_parts[3])], axis=1)
    w_inter = jnp.exp(inter_log - m_t)
    s_b = jnp.concatenate(s_parts, axis=1).astype(BF16)
    num_intra, den_intra = _pair_apply(s_b, v_b, extra=onestack_ref[...])
    yield
    m_new = jnp.maximum(cum_last + m_s, a_end)
    s_prev = jnp.exp(cum_last + m_s - m_new)
    s_loc = jnp.exp(a_end - m_new)
    ct_ref[...] = s_prev * ct + s_loc * c_loc
    n_ref[...] = jnp.broadcast_to(s_prev * n_s + s_loc * n_loc, n_ref.shape)
    m_ref[...] = jnp.broadcast_to(m_new, m_ref.shape)
    num = num_intra + w_inter * q_ct
    den = den_intra + w_inter * q_n
    o_ref[...] = num / jnp.maximum(jnp.abs(den), jnp.exp(-m_t))


def _recurrent_kernel(ugla_ref, uml_ref, uret_ref, cw_ref, tril_ref, e3_ref, bdgla_ref, bd_ref, bdones_ref,
                      onestack_ref, dec_ref, qdec_ref, kdec_ref, gam_ref,
                      ogla_ref, oml_ref, oret_ref,
                      stgla_ref, xbuf_ref, ct_ref, n_ref, m_ref, stret_ref):
    @pl.when(pl.program_id(0) == 0)
    def _():
        for ref in (stgla_ref, xbuf_ref, ct_ref, n_ref, m_ref, stret_ref):
            ref[...] = jnp.zeros_like(ref)

    chains = []
    for b in range(ugla_ref.shape[0]):
        chains += [
            _ml_chunk(uml_ref.at[b], cw_ref, tril_ref, e3_ref, bd_ref, bdones_ref, onestack_ref,
                      oml_ref.at[b], xbuf_ref.at[b], ct_ref.at[b], n_ref.at[b], m_ref.at[b]),
            _gla_chunk(ugla_ref.at[b], tril_ref, bdgla_ref, ogla_ref.at[b], stgla_ref.at[b]),
            _ret_chunk(uret_ref.at[b], dec_ref, qdec_ref, kdec_ref, gam_ref, bd_ref, oret_ref.at[b],
                       stret_ref.at[b])]
    while chains:
        chains = [chain for chain in chains if next(chain, True) is None]


def _merge_kernel(hn_ref, h_ref, ysb_ref, ogla_ref, oml_ref, oret_ref, r_ref, opre_ref, gret_ref,
                  wm_ref, bm_ref, wb_ref, wo_ref, g_ref, ngla_ref, nml_ref, nret_ref, bdmean_ref, o_ref, acc_ref):
    hn = hn_ref[...]
    bdmean = bdmean_ref[...]
    r = r_ref[...]
    g = gret_ref[...]
    branches = [y.astype(BF16) for y in (
        ysb_ref[...],
        _head_rmsnorm(ogla_ref[...], ngla_ref[...], bdmean) * (r * _sigmoid(r)),
        _sigmoid(opre_ref[...]) * _head_rmsnorm(oml_ref[...], nml_ref[...], bdmean),
        (g * _sigmoid(g)) * _head_rmsnorm(oret_ref[...], nret_ref[...], bdmean))]
    for c in range(D_MODEL // MERGE_CHUNK):
        out_cols = slice(MERGE_CHUNK * c, MERGE_CHUNK * (c + 1))
        merged = None
        for n, y_b in enumerate(branches):
            cols = slice(D_MODEL * n + MERGE_CHUNK * c, D_MODEL * n + MERGE_CHUNK * (c + 1))
            gate = _sigmoid(_mm_nt(hn, wm_ref[0, cols, :].astype(BF16)) + bm_ref[:, cols])
            term = gate * _mm(y_b, wb_ref[n, :, out_cols].astype(BF16))
            merged = term if merged is None else merged + term
        part = _mm(merged.astype(BF16), wo_ref[out_cols, :].astype(BF16))
        if c == 0:
            acc_ref[...] = part
        else:
            acc_ref[...] += part
    o_ref[...] = h_ref[...] + _rmsnorm(acc_ref[...], g_ref[...])


def _ffn_kernel(h_ref, gpre_ref, wg_ref, wu_ref, wd_ref, gpost_ref, o_ref, acc_ref):
    x = h_ref[...]
    f = _rmsnorm(x, gpre_ref[...]).astype(BF16)
    for c in range(D_FF // FF_CHUNK):
        cols = slice(FF_CHUNK * c, FF_CHUNK * (c + 1))
        a = _mm(f, wg_ref[:, cols].astype(BF16))
        act = ((a * _sigmoid(a)) * _mm(f, wu_ref[:, cols].astype(BF16))).astype(BF16)
        part = _mm(act, wd_ref[cols, :].astype(BF16))
        if c == 0:
            acc_ref[...] = part
        else:
            acc_ref[...] += part
    o_ref[...] = x + _rmsnorm(acc_ref[...], gpost_ref[...])


def _largest_tile(n, candidates):
    for tm in candidates:
        if n % tm == 0:
            return tm
    raise ValueError(f"{n} rows are not a multiple of {BLOCK}")


def _const_spec(shape, grid_rank):
    zeros = (0,) * len(shape)
    return pl.BlockSpec(shape, (lambda b, j: zeros) if grid_rank == 2 else (lambda n: zeros))


def _params(sem):
    return pltpu.CompilerParams(dimension_semantics=sem, vmem_limit_bytes=V7X_VMEM_LIMIT)


def _row_call(kernel, name, batch, lt, row_inputs, tile_inputs, const_inputs, out_widths, out_dtypes,
              acc_width=None):
    tm = _largest_tile(lt, (640, 512, 384, 256, 128))
    nb = lt // tm
    rows = batch * lt
    row_spec = lambda w, cb: pl.BlockSpec((tm, w), lambda b, j: (b * nb + j, cb))
    return pl.pallas_call(
        kernel,
        grid=(batch, nb),
        in_specs=([row_spec(w, cb) for _, w, cb in row_inputs]
                  + [pl.BlockSpec((tm, a.shape[1]), lambda b, j: (j, 0)) for a in tile_inputs]
                  + [a[1] if isinstance(a, tuple) else _const_spec(a.shape, 2) for a in const_inputs]),
        out_specs=[row_spec(w, 0) for w in out_widths],
        out_shape=[jax.ShapeDtypeStruct((rows, w), dt) for w, dt in zip(out_widths, out_dtypes)],
        scratch_shapes=[] if acc_width is None else [pltpu.VMEM((tm, acc_width), F32)],
        compiler_params=_params(("parallel", "parallel")),
        name=name,
    )(*[a for a, _, _ in row_inputs], *tile_inputs, *[a[0] if isinstance(a, tuple) else a for a in const_inputs])


def _w_in_rows(w_in_t, layer, first, last=None):
    start, stop = _C[first][0], _C[last or first][1]
    block = (pl.Element(1), pl.Element(stop - start), pl.Element(D_MODEL))
    return (w_in_t, pl.BlockSpec(block, lambda b, j: (layer, start, 0), pipeline_mode=pl.Buffered(1)))


def _layer(stacked, layer, single_buffer=False):
    zeros = (0,) * (stacked.ndim - 1)
    mode = pl.Buffered(1) if single_buffer else None
    return (stacked, pl.BlockSpec((None,) + stacked.shape[1:], lambda b, j: (layer,) + zeros, pipeline_mode=mode))


def _full(a):
    return (a, a.shape[1], 0)


def _ffn_call(h, batch, lt, consts, final_seq=None):
    if final_seq is None:
        tm, rows_per_batch, out_rows = _largest_tile(lt, (640, 512, 384, 256, 128)), lt, lt
    else:
        tm, rows_per_batch, out_rows = _largest_tile(final_seq, (512, 256, 128)), final_seq, final_seq
    spec = pl.BlockSpec((None, tm, D_MODEL), lambda b, j: (b, j, 0))
    return pl.pallas_call(
        _ffn_kernel,
        grid=(batch, rows_per_batch // tm),
        in_specs=[spec] + [a[1] if isinstance(a, tuple) else _const_spec(a.shape, 2) for a in consts],
        out_specs=spec,
        out_shape=jax.ShapeDtypeStruct((batch, out_rows, D_MODEL), F32),
        scratch_shapes=[pltpu.VMEM((tm, D_MODEL), F32)],
        compiler_params=_params(("parallel", "parallel")),
        name="swiglu",
    )(h.reshape(batch, lt, D_MODEL), *[a[0] if isinstance(a, tuple) else a for a in consts])


def _logical_to_memory_block(n, nc):
    return jnp.where(n == 0, nc - 1, n - 1)


def _recurrent_call(batch, lt, u_list, const_inputs):
    nc = lt // BLOCK
    u3 = [u.reshape(batch, lt, u.shape[1]) for u in u_list]
    chunk_map = lambda n: (0, _logical_to_memory_block(n, nc), 0)
    o_spec = pl.BlockSpec((batch, BLOCK, BW), chunk_map)
    outs = pl.pallas_call(
        _recurrent_kernel,
        grid=(nc,),
        in_specs=([pl.BlockSpec((batch, BLOCK, u.shape[2]), chunk_map) for u in u3]
                  + [_const_spec(a.shape, 1) for a in const_inputs]),
        out_specs=[o_spec] * 3,
        out_shape=[jax.ShapeDtypeStruct((batch, lt, BW), F32)] * 3,
        scratch_shapes=[pltpu.VMEM((batch, BW, BLOCK), F32),
                        pltpu.VMEM((batch, BLOCK + 8, 2 * BW), F32),
                        pltpu.VMEM((batch, BW, BW), F32),
                        pltpu.VMEM((batch, 8, BW), F32),
                        pltpu.VMEM((batch, 8, BW), F32),
                        pltpu.VMEM((batch, BW, BW), F32)],
        compiler_params=_params(("arbitrary",)),
        name="recurrent_mixers",
    )(*u3, *const_inputs)
    return [o.reshape(batch * lt, BW) for o in outs]


def _sb_call(u_sb, nuo, batch, lt):
    nc = lt // BLOCK
    u3 = u_sb.reshape(batch, lt, U_SB)
    q_spec = pl.BlockSpec((batch, BLOCK, BW), lambda i: (0, _logical_to_memory_block(i, nc), 0))
    return pl.pallas_call(
        _sb_kernel,
        grid=(nc,),
        in_specs=[q_spec,
                  pl.BlockSpec((batch, lt, BW), lambda i: (0, 0, 1)),
                  pl.BlockSpec((batch, lt, BW), lambda i: (0, 0, 2)),
                  _const_spec(nuo.shape, 1)],
        out_specs=q_spec,
        out_shape=jax.ShapeDtypeStruct((batch, lt, BW), F32),
        scratch_shapes=[pltpu.VMEM((batch, BLOCK, BW), F32), pltpu.VMEM((batch, BLOCK, N_HEADS * BLOCK), F32)],
        compiler_params=_params(("parallel",)),
        name="stick_breaking",
    )(u3, u3, u3, nuo).reshape(batch * lt, BW)


def _constants(seq):
    idx = jnp.arange(BLOCK)
    tril = (idx[:, None] >= idx[None, :])
    nuo = -jnp.concatenate([tril, jnp.ones((BLOCK, BLOCK), bool)], axis=1).astype(BF16)
    c = {"tril": jnp.concatenate([tril, tril], axis=1).astype(BF16),
         "nuo": jnp.concatenate([nuo, nuo], axis=0)}
    head_e = jnp.arange(BW) // HEAD_DIM
    c["bd_gla"] = (head_e[:, None] == (jnp.arange(BLOCK) // GLA_DK)[None, :]).astype(F32)
    bd = head_e[:, None] == head_e[None, :]
    c["bd"] = bd.astype(F32)
    c["bdones"] = bd.astype(BF16)
    c["bdmean"] = (bd.astype(F32) / HEAD_DIM).astype(BF16)
    c["onestack"] = ((jnp.arange(2 * BLOCK) // BLOCK)[:, None]
                     == (jnp.arange(BLOCK) // HEAD_DIM)[None, :]).astype(BF16)
    lane = jnp.arange(BLOCK)[:, None]
    spread = jnp.concatenate([lane == I_LANE + head_e[None, :], lane == F_LANE + head_e[None, :]], axis=1)
    c["e3"] = jnp.concatenate([spread] * 3, axis=0).astype(BF16)
    positions = jnp.concatenate([jnp.arange(N_META, seq + N_META), jnp.arange(-N_FRONT_PAD, N_META)])
    half = HEAD_DIM // 2
    inv_freq = ROPE_BASE ** (-jnp.arange(half, dtype=F32) / half)
    ang = positions.astype(jnp.int32).astype(F32)[:, None] * inv_freq[None, :]
    cos, sin = jnp.cos(ang), jnp.sin(ang)
    c["cos"] = jnp.tile(jnp.concatenate([cos, cos], axis=1), (1, N_HEADS))
    c["sin"] = jnp.tile(jnp.concatenate([-sin, sin], axis=1), (1, N_HEADS))
    log_gamma = jnp.log1p(-jnp.exp2(-5.0 - jnp.arange(N_HEADS, dtype=F32)))
    pos = jnp.arange(BLOCK, dtype=F32)
    diff = jnp.maximum(pos[:, None] - pos[None, :], 0.0)
    dec = jnp.where(tril, jnp.exp(log_gamma[:, None, None] * diff), 0.0)
    c["ret_dec"] = jnp.moveaxis(dec, 0, 1).reshape(BLOCK, N_HEADS * BLOCK)
    c["ret_qdec"] = jnp.repeat(jnp.exp(log_gamma[:, None] * (pos + 1.0)).T, HEAD_DIM, axis=1)
    c["ret_kdec"] = jnp.repeat(jnp.exp(log_gamma[:, None] * (BLOCK - 1.0 - pos)).T, HEAD_DIM, axis=1)
    c["ret_gam"] = jnp.repeat(jnp.exp(log_gamma * BLOCK), HEAD_DIM)[None, :]
    return c


def _small_group(w_in_t):
    rows = lambda name: w_in_t[:, _C[name][0]:_C[name][1], :]
    pad = jnp.zeros((w_in_t.shape[0], BLOCK - GLA_LOW_RANK - 2 * N_HEADS, D_MODEL), w_in_t.dtype)
    return jnp.concatenate([rows("gla_code"), rows("ml_i"), rows("ml_f"), pad], axis=1)


def kernel(x, meta_tokens, norm_mix_pre, norm_mix_post, norm_ffn_pre, norm_ffn_post, w_in,
           gla_w_gate_up, gla_b_gate, gla_norm, ml_conv, ml_b_i, ml_b_f, ml_norm, ret_norm,
           w_branch, b_merge, w_out, ffn_w_gate, ffn_w_up, ffn_w_down):
    batch, seq, d = x.shape
    depth = w_in.shape[0]
    lt = seq + BLOCK
    pad = jnp.zeros((batch, N_FRONT_PAD, d), x.dtype)
    meta = jnp.broadcast_to(meta_tokens.astype(x.dtype)[None], (batch, N_META, d))
    h = jnp.concatenate([x, pad, meta], axis=1).reshape(batch * lt, d)

    c = _constants(seq)
    w_in_t = jnp.swapaxes(w_in, 1, 2)
    w_small = _small_group(w_in_t)
    wg_gla = jnp.pad(gla_w_gate_up, ((0, 0), (0, BLOCK - GLA_LOW_RANK), (0, 0))).astype(BF16)
    tail = BLOCK - GLA_LOW_RANK - 2 * N_HEADS
    b_gate = jnp.pad(jnp.concatenate([ml_b_i, ml_b_f], axis=-1), ((0, 0), (GLA_LOW_RANK, tail)))
    wb, wo = w_branch, w_out
    wfg, wfu, wfd = ffn_w_gate, ffn_w_up, ffn_w_down
    tm = _largest_tile(lt, (640, 512, 384, 256, 128))

    for l in range(depth):
        hn, u_sb, u_gla, u_ml, u_ret = _row_call(
            functools.partial(_proj_kernel, tm=tm, seq=seq), "in_projection", batch, lt,
            [_full(h)], [c["cos"], c["sin"]],
            [norm_mix_pre[l][None], _w_in_rows(w_in_t, l, "sb"), _w_in_rows(w_in_t, l, "gla_q", "gla_r"),
             _w_in_rows(w_in_t, l, "ml_qk"), _w_in_rows(w_in_t, l, "ml_v"), _w_in_rows(w_in_t, l, "ml_o"),
             _w_in_rows(w_in_t, l, "ret"), _layer(w_small, l), _layer(wg_gla, l),
             gla_b_gate[l][None], b_gate[l][None]],
            [D_MODEL, U_SB, U_GLA, U_ML, U_RET], [BF16, BF16, F32, F32, F32])
        y_sb = _sb_call(u_sb, c["nuo"], batch, lt)
        o_gla, o_ml, o_ret = _recurrent_call(
            batch, lt, [u_gla, u_ml, u_ret],
            [ml_conv[l], c["tril"], c["e3"], c["bd_gla"], c["bd"], c["bdones"], c["onestack"],
             c["ret_dec"], c["ret_qdec"], c["ret_kdec"], c["ret_gam"]])
        (h,) = _row_call(
            _merge_kernel, "merge", batch, lt,
            [_full(hn), _full(h), _full(y_sb), _full(o_gla), _full(o_ml), _full(o_ret),
             (u_gla, BW, 2), (u_ml, BW, 3), (u_ret, BW, 3)], [],
            [_w_in_rows(w_in_t, l, "merge"), b_merge[l].reshape(1, N_BRANCH * D_MODEL), _layer(wb, l, True),
             _layer(wo, l, True),
             norm_mix_post[l][None],
             gla_norm[l][None], ml_norm[l][None], ret_norm[l][None], c["bdmean"]],
            [D_MODEL], [F32], acc_width=D_MODEL)
        ffn_consts = [norm_ffn_pre[l][None], _layer(wfg, l, True), _layer(wfu, l, True), _layer(wfd, l, True),
                      norm_ffn_post[l][None]]
        if l + 1 < depth:
            h = _ffn_call(h, batch, lt, ffn_consts).reshape(batch * lt, d)
        else:
            return _ffn_call(h, batch, lt, ffn_consts, final_seq=seq)
```

```python
import functools

import jax
import jax.numpy as jnp
from jax import lax
from jax.experimental import pallas as pl
from jax.experimental.pallas import tpu as pltpu

F32 = jnp.float32
BF16 = jnp.bfloat16

D_MODEL = 1024
BLOCK = 128
HEAD_DIM = 64
N_HEADS = 4
BW = N_HEADS * HEAD_DIM
N_BRANCH = 4
N_META = 16
N_FRONT_PAD = BLOCK - N_META
GLA_DK = 32
GLA_LOW_RANK = 16
GLA_GATE_NORMALIZER = 16.0
CONV_WIDTH = 4
D_FF = 2816
FF_CHUNK = 256
MERGE_CHUNK = 256
ROPE_BASE = 10000.0
NORM_EPS = 1e-6
SB_LOG_ZERO = -104.0
SB_STATIC_TILES = 3
V7X_VMEM_LIMIT = 56 * 1024 * 1024

CODE_LANE = 0
I_LANE = GLA_LOW_RANK
F_LANE = GLA_LOW_RANK + N_HEADS

_C = {}
_off = 0
for _name, _w in (("sb", 768), ("gla_q", 128), ("gla_k", 128), ("gla_v", 256), ("gla_r", 256), ("gla_code", 16),
                  ("ml_qk", 512), ("ml_v", 256), ("ml_i", 4), ("ml_f", 4), ("ml_o", 256), ("ret", 1024),
                  ("merge", 4096)):
    _C[_name] = (_off, _off + _w)
    _off += _w
IN_WIDTH = _off

U_SB, U_GLA, U_ML, U_RET = 768, 896, 1152, 1024


def _mm(a, b):
    return jnp.dot(a, b, preferred_element_type=F32)


def _mm_nt(a, b):
    return lax.dot_general(a, b, (((1,), (1,)), ((), ())), preferred_element_type=F32)


def _split(x):
    hi = x.astype(BF16)
    lo = (x - hi.astype(F32)).astype(BF16)
    return hi, lo


def _mm_split_r(c2, x):
    hi, lo = _split(x)
    return _mm(c2, jnp.concatenate([hi, lo], axis=0))


def _softplus(x):
    return jnp.maximum(x, 0.0) + jnp.log(1.0 + jnp.exp(-jnp.abs(x)))


def _log_sigmoid(x):
    return -_softplus(-x)


def _sigmoid(x):
    return 1.0 / (1.0 + jnp.exp(-x))


def _rmsnorm(x, g):
    return x * lax.rsqrt(jnp.mean(x * x, axis=-1, keepdims=True) + NORM_EPS) * g


def _head_rmsnorm(y, g, bdmean):
    ms = _mm((y * y).astype(BF16), bdmean)
    return y * lax.rsqrt(ms + NORM_EPS) * g


def _lanes_in(width, lo, hi):
    lane = lax.broadcasted_iota(jnp.int32, (1, width), 1)
    return (lane >= lo) & (lane < hi)


def _row_stack_heads(x, per_head, n=N_HEADS):
    keep = [_lanes_in(x.shape[1], per_head * h, per_head * (h + 1)).astype(x.dtype) for h in range(n)]
    return jnp.concatenate([x * m for m in keep], axis=0)


def _pair_scores(q_b, k_b):
    zs = []
    for p in range(N_HEADS // 2):
        lanes = slice(BLOCK * p, BLOCK * (p + 1))
        zs.append(_mm_nt(q_b[:, lanes], _row_stack_heads(k_b[:, lanes], HEAD_DIM, 2)))
    return jnp.concatenate(zs, axis=1)


def _pair_apply(s_b, v_b, extra=None):
    outs, extras = [], []
    for p in range(N_HEADS // 2):
        rhs = _row_stack_heads(v_b[:, BLOCK * p:BLOCK * (p + 1)], HEAD_DIM, 2)
        if extra is not None:
            rhs = jnp.concatenate([rhs, extra], axis=1)
        r = _mm(s_b[:, 2 * BLOCK * p:2 * BLOCK * (p + 1)], rhs)
        outs.append(r[:, :BLOCK])
        extras.append(r[:, BLOCK:])
    out = jnp.concatenate(outs, axis=1)
    return out if extra is None else (out, jnp.concatenate(extras, axis=1))


def _tile4(x):
    return jnp.concatenate([x] * N_HEADS, axis=1)


def _rotary(x, cos, sin_signed):
    first_half = _lanes_in(BLOCK, 0, 32) | _lanes_in(BLOCK, 64, 96)
    outs = []
    for t in range(2):
        lanes = slice(BLOCK * t, BLOCK * (t + 1))
        xt = x[:, lanes]
        swapped = jnp.where(first_half, pltpu.roll(xt, 96, 1), pltpu.roll(xt, 32, 1))
        outs.append(xt * cos[:, lanes] + swapped * sin_signed[:, lanes])
    return jnp.concatenate(outs, axis=1)


def _proj_kernel(h_ref, cos_ref, sin_ref, g_ref, wsb_ref, wgla_ref, wmlqk_ref, wmlv_ref, wmlo_ref, wret_ref,
                 wsmall_ref, wg_ref, bg_ref, bgate_ref, hn_ref, usb_ref, ugla_ref, uml_ref, uret_ref, *, tm, seq):
    x = h_ref[...]
    pos = pl.program_id(1) * tm + lax.broadcasted_iota(jnp.int32, (tm, 1), 0)
    valid = ((pos < seq) | (pos >= seq + N_FRONT_PAD)).astype(F32)
    hn = (_rmsnorm(x, g_ref[...]) * valid).astype(BF16)
    hn_ref[...] = hn
    proj = lambda w_ref: _mm_nt(hn, w_ref[0].astype(BF16))
    small = _mm_nt(hn, wsmall_ref[...].astype(BF16))
    ret = proj(wret_ref)
    log_a_pre = _mm(small.astype(BF16), wg_ref[...])
    usb_ref[...] = proj(wsb_ref).astype(BF16)
    cos = cos_ref[...]
    sin = sin_ref[...]
    uret_ref[:, 0:256] = _rotary(ret[:, 0:256], cos, sin)
    uret_ref[:, 256:512] = _rotary(ret[:, 256:512], cos, sin) * (HEAD_DIM ** -0.5)
    uret_ref[:, 512:1024] = ret[:, 512:1024]
    ugla_ref[:, 768:896] = _log_sigmoid(log_a_pre + bg_ref[...]) * (1.0 / GLA_GATE_NORMALIZER)
    gates = small + bgate_ref[...]
    uml_ref[:, 1024:1152] = jnp.where(_lanes_in(BLOCK, F_LANE, F_LANE + N_HEADS), _log_sigmoid(gates), gates)
    ugla_ref[:, 0:768] = proj(wgla_ref)
    uml_ref[:, 0:512] = proj(wmlqk_ref)
    uml_ref[:, 512:768] = proj(wmlv_ref)
    uml_ref[:, 768:1024] = proj(wmlo_ref)


def _sb_tile(q, k_ref, v_ref, j, n_blocks, bias, nuo2, cb):
    state = {"cb": cb, "out": None}
    for _ in _sb_tile_stages(q, k_ref, v_ref, j, n_blocks, bias, nuo2, state):
        pass
    return state["out"], state["cb"]


def _sb_tile_stages(q, k_ref, v_ref, j, n_blocks, bias, nuo2, state):
    start = pl.multiple_of(jnp.where(j <= 0, n_blocks - 1, j - 1) * BLOCK, BLOCK)
    kj = k_ref[pl.ds(start, BLOCK), :]
    zs = []
    for p in range(N_HEADS // 2):
        lanes = slice(BLOCK * p, BLOCK * (p + 1))
        k_rows = _row_stack_heads(kj[:, lanes], HEAD_DIM, 2)
        zs.append(_mm_nt(q[:, lanes], k_rows))
    yield
    z = jnp.concatenate(zs, axis=1) + bias
    hi, lo = _split(_softplus(z))
    rs = []
    for h in range(N_HEADS):
        lanes = slice(BLOCK * h, BLOCK * (h + 1))
        rs.append(_mm(jnp.concatenate([hi[:, lanes], lo[:, lanes]], axis=1), nuo2))
    yield
    cb = state["cb"]
    a = jnp.exp(z + jnp.concatenate([r[:, :BLOCK] for r in rs], axis=1) + cb).astype(BF16)
    state["cb"] = cb + jnp.concatenate([r[:, BLOCK:] for r in rs], axis=1)
    contrib = _mm(a, _row_stack_heads(v_ref[pl.ds(start, BLOCK), :], HEAD_DIM))
    yield
    acc_ref = state.get("acc_ref")
    if acc_ref is None:
        state["out"] = contrib if state["out"] is None else state["out"] + contrib
    elif state["out"] is None:
        acc_ref[...] = contrib
        state["out"] = acc_ref
    else:
        acc_ref[...] += contrib
    yield


def _sb_kernel(q_ref, k_ref, v_ref, nuo_ref, o_ref, acc_ref, cb_ref):
    i = pl.program_id(0)
    n_blocks = pl.num_programs(0)
    batch = q_ref.shape[0]
    qs = [q_ref[b] * jnp.asarray(HEAD_DIM ** -0.5, BF16) for b in range(batch)]
    row = lax.broadcasted_iota(jnp.int32, (BLOCK, BLOCK), 0)
    col = lax.broadcasted_iota(jnp.int32, (BLOCK, BLOCK), 1)
    nuo2 = nuo_ref[...]

    def bias_for(j, causal):
        ok = (j * BLOCK + col) >= N_FRONT_PAD
        if causal:
            ok = ok & (col < row)
        return jnp.concatenate([jnp.where(ok, 0.0, -jnp.inf)] * N_HEADS, axis=1)

    def all_max(x):
        return jnp.max(jnp.max(x, axis=1, keepdims=True), axis=0, keepdims=True)[0, 0]

    biases = [bias_for(i - t, t == 0) for t in range(SB_STATIC_TILES)]
    states = [{"cb": jnp.zeros((BLOCK, N_HEADS * BLOCK), F32), "out": None, "acc_ref": acc_ref.at[b]}
              for b in range(batch)]
    tiles = [_sb_tile_stages(qs[b], k_ref.at[b], v_ref.at[b], i - t, n_blocks, biases[t], nuo2, states[b])
             for t in range(SB_STATIC_TILES) for b in range(batch)]
    for _ in range(4):
        for tile in tiles:
            next(tile)

    def cond(c):
        j, mx = c
        return jnp.logical_and(j >= 0, mx > SB_LOG_ZERO)

    for b in range(batch):
        cb_ref[b] = states[b]["cb"]

        def body(c, b=b):
            j, _ = c
            contrib, cbn = _sb_tile(qs[b], k_ref.at[b], v_ref.at[b], j, n_blocks, bias_for(j, False), nuo2,
                                    cb_ref[b])
            acc_ref[b] += contrib
            cb_ref[b] = cbn
            return j - 1, all_max(cbn)

        lax.while_loop(cond, body, (i - SB_STATIC_TILES, all_max(states[b]["cb"])))
        o_ref[b] = acc_ref[b]


def _gla_chunk(u_ref, tril_ref, bd_ref, o_ref, st_ref):
    q = u_ref[:, 0:128] * (GLA_DK ** -0.5)
    k = u_ref[:, 128:256]
    v = u_ref[:, 256:512]
    log_a = u_ref[:, 768:896]
    cum = _mm_split_r(tril_ref[...], log_a)
    yield
    cum_last = cum[BLOCK - 1:BLOCK, :]
    qd = (q * jnp.exp(cum)).astype(BF16)
    kd = (k * jnp.exp(-cum)).astype(BF16)
    ke = (k * jnp.exp(cum_last - cum)).astype(BF16)
    st = st_ref[...]
    inter = _mm_nt(qd, st.astype(BF16))
    scores = _mm_nt(qd, _row_stack_heads(kd, GLA_DK))
    local = _mm(v.T.astype(BF16), ke)
    yield
    row = lax.broadcasted_iota(jnp.int32, (BLOCK, BLOCK), 0)
    col = lax.broadcasted_iota(jnp.int32, (BLOCK, BLOCK), 1)
    s = jnp.where(_tile4(row >= col), scores, 0.0)
    intra = _pair_apply(s.astype(BF16), v.astype(BF16))
    yield
    st_ref[...] = st * jnp.exp(cum_last) + local * bd_ref[...]
    o_ref[...] = intra + inter


def _ret_chunk(u_ref, dec_ref, qdec_ref, kdec_ref, gam_ref, bd_ref, o_ref, st_ref):
    qr = u_ref[:, 0:256]
    kr = u_ref[:, 256:512]
    v = u_ref[:, 512:768]
    scores = _pair_scores(qr.astype(BF16), kr.astype(BF16))
    st = st_ref[...]
    inter = _mm_nt((qr * qdec_ref[...]).astype(BF16), st.astype(BF16))
    local = _mm(v.T.astype(BF16), (kr * kdec_ref[...]).astype(BF16))
    yield
    s = scores * dec_ref[...]
    intra = _pair_apply(s.astype(BF16), v.astype(BF16))
    yield
    st_ref[...] = st * gam_ref[...] + local * bd_ref[...]
    o_ref[...] = intra + inter


def _spread_head(x_t, h):
    own = _lanes_in(BLOCK, 0, HEAD_DIM) if h % 2 == 0 else _lanes_in(BLOCK, HEAD_DIM, BLOCK)
    return jnp.where(own, x_t, pltpu.roll(x_t, HEAD_DIM, 1))


def _ml_chunk(u_ref, cw_ref, tril_ref, e3_ref, bd_ref, bdones_ref, onestack_ref, o_ref, xbuf_ref, ct_ref, n_ref,
              m_ref):
    gates = u_ref[:, 1024:1152]
    gates_cum = _mm_split_r(tril_ref[...], gates)
    xbuf_ref[8:8 + BLOCK, :] = u_ref[:, 0:512]
    conv = None
    for j in range(CONV_WIDTH):
        term = cw_ref[j:j + 1, :] * xbuf_ref[8 - (CONV_WIDTH - 1) + j:8 - (CONV_WIDTH - 1) + j + BLOCK, :]
        conv = term if conv is None else conv + term
    xbuf_ref[0:8, :] = xbuf_ref[BLOCK:BLOCK + 8, :]
    yield
    compact = jnp.where(_lanes_in(BLOCK, F_LANE, F_LANE + N_HEADS), gates_cum, gates)
    t1 = compact.astype(BF16)
    rest = compact - t1.astype(F32)
    t2 = rest.astype(BF16)
    t3 = (rest - t2.astype(F32)).astype(BF16)
    spread = _mm(jnp.concatenate([t1, t2, t3], axis=1), e3_ref[...])
    qk = conv * _sigmoid(conv)
    q = qk[:, 0:BW]
    k = qk[:, BW:2 * BW] * (HEAD_DIM ** -0.5)
    v = u_ref[:, 512:768]
    q_b = q.astype(BF16)
    k_b = k.astype(BF16)
    qk_all = _pair_scores(q_b, k_b)
    m_s = m_ref[0:1, :]
    n_s = n_ref[0:1, :]
    ct = ct_ref[...]
    q_ct = _mm_nt(q_b, ct.astype(BF16))
    q_n = _mm((q * n_s).astype(BF16), bdones_ref[...])
    yield
    log_i = spread[:, 0:BW]
    cum = spread[:, BW:2 * BW]

    cum_last = cum[BLOCK - 1:BLOCK, :]
    w_end = cum_last - cum + log_i
    a_end = jnp.max(w_end, axis=0, keepdims=True)
    kp = k * jnp.exp(w_end - a_end)
    v_b = v.astype(BF16)
    c_loc = _mm(v.T.astype(BF16), kp.astype(BF16)) * bd_ref[...]
    n_loc = jnp.sum(kp, axis=0, keepdims=True)

    inter_log = cum + m_s
    d_t = (log_i - cum).T
    yield
    row = lax.broadcasted_iota(jnp.int32, (BLOCK, BLOCK), 0)
    col = lax.broadcasted_iota(jnp.int32, (BLOCK, BLOCK), 1)
    s_parts, m_parts = [], []
    for h in range(N_HEADS):
        if h == N_HEADS // 2:
            yield
        t = h // 2
        cq = _spread_head(cum[:, BLOCK * t:BLOCK * (t + 1)], h)
        ilq = _spread_head(inter_log[:, BLOCK * t:BLOCK * (t + 1)], h)
        dk = jnp.broadcast_to(d_t[HEAD_DIM * h:HEAD_DIM * h + 1, :], (BLOCK, BLOCK))
        intra_log = jnp.where(row >= col, cq + dk, -jnp.inf)
        m_h = jnp.maximum(ilq, jnp.max(intra_log, axis=1, keepdims=True))
        s_parts.append(qk_all[:, BLOCK * h:BLOCK * (h + 1)] * jnp.exp(intra_log - m_h))
        m_parts.append(m_h)
    first = _lanes_in(BLOCK, 0, HEAD_DIM)
    m_t = jnp.concatenate([jnp.where(first, m_parts[0], m_parts[1]),
                           jnp.where(first, m_parts[2], m_parts[3])], axis=1)
    w_inter = jnp.exp(inter_log - m_t)
    s_b = jnp.concatenate(s_parts, axis=1).astype(BF16)
    num_intra, den_intra = _pair_apply(s_b, v_b, extra=onestack_ref[...])
    yield
    m_new = jnp.maximum(cum_last + m_s, a_end)
    s_prev = jnp.exp(cum_last + m_s - m_new)
    s_loc = jnp.exp(a_end - m_new)
    ct_ref[...] = s_prev * ct + s_loc * c_loc
    n_ref[...] = jnp.broadcast_to(s_prev * n_s + s_loc * n_loc, n_ref.shape)
    m_ref[...] = jnp.broadcast_to(m_new, m_ref.shape)
    num = num_intra + w_inter * q_ct
    den = den_intra + w_inter * q_n
    o_ref[...] = num / jnp.maximum(jnp.abs(den), jnp.exp(-m_t))


def _recurrent_kernel(ugla_ref, uml_ref, uret_ref, cw_ref, tril_ref, e3_ref, bdgla_ref, bd_ref, bdones_ref,
                      onestack_ref, dec_ref, qdec_ref, kdec_ref, gam_ref,
                      ogla_ref, oml_ref, oret_ref,
                      stgla_ref, xbuf_ref, ct_ref, n_ref, m_ref, stret_ref):
    @pl.when(pl.program_id(0) == 0)
    def _():
        for ref in (stgla_ref, xbuf_ref, ct_ref, n_ref, m_ref, stret_ref):
            ref[...] = jnp.zeros_like(ref)

    chains = []
    for b in range(ugla_ref.shape[0]):
        chains += [
            _ml_chunk(uml_ref.at[b], cw_ref, tril_ref, e3_ref, bd_ref, bdones_ref, onestack_ref,
                      oml_ref.at[b], xbuf_ref.at[b], ct_ref.at[b], n_ref.at[b], m_ref.at[b]),
            _gla_chunk(ugla_ref.at[b], tril_ref, bdgla_ref, ogla_ref.at[b], stgla_ref.at[b]),
            _ret_chunk(uret_ref.at[b], dec_ref, qdec_ref, kdec_ref, gam_ref, bd_ref, oret_ref.at[b],
                       stret_ref.at[b])]
    while chains:
        chains = [chain for chain in chains if next(chain, True) is None]


def _merge_kernel(hn_ref, h_ref, ysb_ref, ogla_ref, oml_ref, oret_ref, r_ref, opre_ref, gret_ref,
                  wm_ref, bm_ref, wb_ref, wo_ref, g_ref, ngla_ref, nml_ref, nret_ref, bdmean_ref, o_ref, acc_ref):
    hn = hn_ref[...]
    bdmean = bdmean_ref[...]
    r = r_ref[...]
    g = gret_ref[...]
    branches = [y.astype(BF16) for y in (
        ysb_ref[...],
        _head_rmsnorm(ogla_ref[...], ngla_ref[...], bdmean) * (r * _sigmoid(r)),
        _sigmoid(opre_ref[...]) * _head_rmsnorm(oml_ref[...], nml_ref[...], bdmean),
        (g * _sigmoid(g)) * _head_rmsnorm(oret_ref[...], nret_ref[...], bdmean))]
    for c in range(D_MODEL // MERGE_CHUNK):
        out_cols = slice(MERGE_CHUNK * c, MERGE_CHUNK * (c + 1))
        merged = None
        for n, y_b in enumerate(branches):
            cols = slice(D_MODEL * n + MERGE_CHUNK * c, D_MODEL * n + MERGE_CHUNK * (c + 1))
            gate = _sigmoid(_mm_nt(hn, wm_ref[0, cols, :].astype(BF16)) + bm_ref[:, cols])
            term = gate * _mm(y_b, wb_ref[n, :, out_cols].astype(BF16))
            merged = term if merged is None else merged + term
        part = _mm(merged.astype(BF16), wo_ref[out_cols, :].astype(BF16))
        if c == 0:
            acc_ref[...] = part
        else:
            acc_ref[...] += part
    o_ref[...] = h_ref[...] + _rmsnorm(acc_ref[...], g_ref[...])


def _ffn_kernel(h_ref, gpre_ref, wg_ref, wu_ref, wd_ref, gpost_ref, o_ref, acc_ref):
    x = h_ref[...]
    f = _rmsnorm(x, gpre_ref[...]).astype(BF16)
    for c in range(D_FF // FF_CHUNK):
        cols = slice(FF_CHUNK * c, FF_CHUNK * (c + 1))
        a = _mm(f, wg_ref[:, cols].astype(BF16))
        act = ((a * _sigmoid(a)) * _mm(f, wu_ref[:, cols].astype(BF16))).astype(BF16)
        part = _mm(act, wd_ref[cols, :].astype(BF16))
        if c == 0:
            acc_ref[...] = part
        else:
            acc_ref[...] += part
    o_ref[...] = x + _rmsnorm(acc_ref[...], gpost_ref[...])


def _largest_tile(n, candidates):
    for tm in candidates:
        if n % tm == 0:
            return tm
    raise ValueError(f"{n} rows are not a multiple of {BLOCK}")


def _const_spec(shape, grid_rank):
    zeros = (0,) * len(shape)
    return pl.BlockSpec(shape, (lambda b, j: zeros) if grid_rank == 2 else (lambda n: zeros))


def _params(sem):
    return pltpu.CompilerParams(dimension_semantics=sem, vmem_limit_bytes=V7X_VMEM_LIMIT)


def _row_call(kernel, name, batch, lt, row_inputs, tile_inputs, const_inputs, out_widths, out_dtypes,
              acc_width=None):
    tm = _largest_tile(lt, (640, 512, 384, 256, 128))
    nb = lt // tm
    rows = batch * lt
    row_spec = lambda w, cb: pl.BlockSpec((tm, w), lambda b, j: (b * nb + j, cb))
    return pl.pallas_call(
        kernel,
        grid=(batch, nb),
        in_specs=([row_spec(w, cb) for _, w, cb in row_inputs]
                  + [pl.BlockSpec((tm, a.shape[1]), lambda b, j: (j, 0)) for a in tile_inputs]
                  + [a[1] if isinstance(a, tuple) else _const_spec(a.shape, 2) for a in const_inputs]),
        out_specs=[row_spec(w, 0) for w in out_widths],
        out_shape=[jax.ShapeDtypeStruct((rows, w), dt) for w, dt in zip(out_widths, out_dtypes)],
        scratch_shapes=[] if acc_width is None else [pltpu.VMEM((tm, acc_width), F32)],
        compiler_params=_params(("parallel", "parallel")),
        name=name,
    )(*[a for a, _, _ in row_inputs], *tile_inputs, *[a[0] if isinstance(a, tuple) else a for a in const_inputs])


def _w_in_rows(w_in_t, layer, first, last=None):
    start, stop = _C[first][0], _C[last or first][1]
    block = (pl.Element(1), pl.Element(stop - start), pl.Element(D_MODEL))
    return (w_in_t, pl.BlockSpec(block, lambda b, j: (layer, start, 0), pipeline_mode=pl.Buffered(1)))


def _layer(stacked, layer, single_buffer=False):
    zeros = (0,) * (stacked.ndim - 1)
    mode = pl.Buffered(1) if single_buffer else None
    return (stacked, pl.BlockSpec((None,) + stacked.shape[1:], lambda b, j: (layer,) + zeros, pipeline_mode=mode))


def _full(a):
    return (a, a.shape[1], 0)


def _ffn_call(h, batch, lt, consts, final_seq=None):
    if final_seq is None:
        tm, rows_per_batch, out_rows = _largest_tile(lt, (640, 512, 384, 256, 128)), lt, lt
    else:
        tm, rows_per_batch, out_rows = _largest_tile(final_seq, (512, 256, 128)), final_seq, final_seq
    spec = pl.BlockSpec((None, tm, D_MODEL), lambda b, j: (b, j, 0))
    return pl.pallas_call(
        _ffn_kernel,
        grid=(batch, rows_per_batch // tm),
        in_specs=[spec] + [a[1] if isinstance(a, tuple) else _const_spec(a.shape, 2) for a in consts],
        out_specs=spec,
        out_shape=jax.ShapeDtypeStruct((batch, out_rows, D_MODEL), F32),
        scratch_shapes=[pltpu.VMEM((tm, D_MODEL), F32)],
        compiler_params=_params(("parallel", "parallel")),
        name="swiglu",
    )(h.reshape(batch, lt, D_MODEL), *[a[0] if isinstance(a, tuple) else a for a in consts])


def _logical_to_memory_block(n, nc):
    return jnp.where(n == 0, nc - 1, n - 1)


def _recurrent_call(batch, lt, u_list, const_inputs):
    nc = lt // BLOCK
    u3 = [u.reshape(batch, lt, u.shape[1]) for u in u_list]
    chunk_map = lambda n: (0, _logical_to_memory_block(n, nc), 0)
    o_spec = pl.BlockSpec((batch, BLOCK, BW), chunk_map)
    outs = pl.pallas_call(
        _recurrent_kernel,
        grid=(nc,),
        in_specs=([pl.BlockSpec((batch, BLOCK, u.shape[2]), chunk_map) for u in u3]
                  + [_const_spec(a.shape, 1) for a in const_inputs]),
        out_specs=[o_spec] * 3,
        out_shape=[jax.ShapeDtypeStruct((batch, lt, BW), F32)] * 3,
        scratch_shapes=[pltpu.VMEM((batch, BW, BLOCK), F32),
                        pltpu.VMEM((batch, BLOCK + 8, 2 * BW), F32),
                        pltpu.VMEM((batch, BW, BW), F32),
                        pltpu.VMEM((batch, 8, BW), F32),
                        pltpu.VMEM((batch, 8, BW), F32),
                        pltpu.VMEM((batch, BW, BW), F32)],
        compiler_params=_params(("arbitrary",)),
        name="recurrent_mixers",
    )(*u3, *const_inputs)
    return [o.reshape(batch * lt, BW) for o in outs]


def _sb_call(u_sb, nuo, batch, lt):
    nc = lt // BLOCK
    u3 = u_sb.reshape(batch, lt, U_SB)
    q_spec = pl.BlockSpec((batch, BLOCK, BW), lambda i: (0, _logical_to_memory_block(i, nc), 0))
    return pl.pallas_call(
        _sb_kernel,
        grid=(nc,),
        in_specs=[q_spec,
                  pl.BlockSpec((batch, lt, BW), lambda i: (0, 0, 1)),
                  pl.BlockSpec((batch, lt, BW), lambda i: (0, 0, 2)),
                  _const_spec(nuo.shape, 1)],
        out_specs=q_spec,
        out_shape=jax.ShapeDtypeStruct((batch, lt, BW), F32),
        scratch_shapes=[pltpu.VMEM((batch, BLOCK, BW), F32), pltpu.VMEM((batch, BLOCK, N_HEADS * BLOCK), F32)],
        compiler_params=_params(("parallel",)),
        name="stick_breaking",
    )(u3, u3, u3, nuo).reshape(batch * lt, BW)


def _constants(seq):
    idx = jnp.arange(BLOCK)
    tril = (idx[:, None] >= idx[None, :])
    nuo = -jnp.concatenate([tril, jnp.ones((BLOCK, BLOCK), bool)], axis=1).astype(BF16)
    c = {"tril": jnp.concatenate([tril, tril], axis=1).astype(BF16),
         "nuo": jnp.concatenate([nuo, nuo], axis=0)}
    head_e = jnp.arange(BW) // HEAD_DIM
    c["bd_gla"] = (head_e[:, None] == (jnp.arange(BLOCK) // GLA_DK)[None, :]).astype(F32)
    bd = head_e[:, None] == head_e[None, :]
    c["bd"] = bd.astype(F32)
    c["bdones"] = bd.astype(BF16)
    c["bdmean"] = (bd.astype(F32) / HEAD_DIM).astype(BF16)
    c["onestack"] = ((jnp.arange(2 * BLOCK) // BLOCK)[:, None]
                     == (jnp.arange(BLOCK) // HEAD_DIM)[None, :]).astype(BF16)
    lane = jnp.arange(BLOCK)[:, None]
    spread = jnp.concatenate([lane == I_LANE + head_e[None, :], lane == F_LANE + head_e[None, :]], axis=1)
    c["e3"] = jnp.concatenate([spread] * 3, axis=0).astype(BF16)
    positions = jnp.concatenate([jnp.arange(N_META, seq + N_META), jnp.arange(-N_FRONT_PAD, N_META)])
    half = HEAD_DIM // 2
    inv_freq = ROPE_BASE ** (-jnp.arange(half, dtype=F32) / half)
    ang = positions.astype(jnp.int32).astype(F32)[:, None] * inv_freq[None, :]
    lane_freq = jnp.arange(BW) % half
    spread = (jnp.arange(half)[:, None] == lane_freq[None, :]).astype(F32)
    sign = jnp.where((jnp.arange(BW) % HEAD_DIM) < half, -1.0, 1.0).astype(F32)
    c["cos"] = jnp.dot(jnp.cos(ang), spread, precision=lax.Precision.HIGHEST)
    c["sin"] = jnp.dot(jnp.sin(ang), spread * sign[None, :], precision=lax.Precision.HIGHEST)
    log_gamma = jnp.log1p(-jnp.exp2(-5.0 - jnp.arange(N_HEADS, dtype=F32)))
    pos = jnp.arange(BLOCK, dtype=F32)
    diff = jnp.maximum(pos[:, None] - pos[None, :], 0.0)
    dec = jnp.where(tril, jnp.exp(log_gamma[:, None, None] * diff), 0.0)
    c["ret_dec"] = jnp.moveaxis(dec, 0, 1).reshape(BLOCK, N_HEADS * BLOCK)
    c["ret_qdec"] = jnp.repeat(jnp.exp(log_gamma[:, None] * (pos + 1.0)).T, HEAD_DIM, axis=1)
    c["ret_kdec"] = jnp.repeat(jnp.exp(log_gamma[:, None] * (BLOCK - 1.0 - pos)).T, HEAD_DIM, axis=1)
    c["ret_gam"] = jnp.repeat(jnp.exp(log_gamma * BLOCK), HEAD_DIM)[None, :]
    return c


def _small_group(w_in_t):
    rows = lambda name: w_in_t[:, _C[name][0]:_C[name][1], :]
    pad = jnp.zeros((w_in_t.shape[0], BLOCK - GLA_LOW_RANK - 2 * N_HEADS, D_MODEL), w_in_t.dtype)
    return jnp.concatenate([rows("gla_code"), rows("ml_i"), rows("ml_f"), pad], axis=1)


def kernel(x, meta_tokens, norm_mix_pre, norm_mix_post, norm_ffn_pre, norm_ffn_post, w_in,
           gla_w_gate_up, gla_b_gate, gla_norm, ml_conv, ml_b_i, ml_b_f, ml_norm, ret_norm,
           w_branch, b_merge, w_out, ffn_w_gate, ffn_w_up, ffn_w_down):
    batch, seq, d = x.shape
    depth = w_in.shape[0]
    lt = seq + BLOCK
    pad = jnp.zeros((batch, N_FRONT_PAD, d), x.dtype)
    meta = jnp.broadcast_to(meta_tokens.astype(x.dtype)[None], (batch, N_META, d))
    h = jnp.concatenate([x, pad, meta], axis=1).reshape(batch * lt, d)

    c = _constants(seq)
    w_in_t = jnp.swapaxes(w_in, 1, 2)
    w_small = _small_group(w_in_t)
    wg_gla = jnp.pad(gla_w_gate_up, ((0, 0), (0, BLOCK - GLA_LOW_RANK), (0, 0))).astype(BF16)
    tail = BLOCK - GLA_LOW_RANK - 2 * N_HEADS
    b_gate = jnp.pad(jnp.concatenate([ml_b_i, ml_b_f], axis=-1), ((0, 0), (GLA_LOW_RANK, tail)))
    wb, wo = w_branch, w_out
    wfg, wfu, wfd = ffn_w_gate, ffn_w_up, ffn_w_down
    tm = _largest_tile(lt, (640, 512, 384, 256, 128))

    for l in range(depth):
        hn, u_sb, u_gla, u_ml, u_ret = _row_call(
            functools.partial(_proj_kernel, tm=tm, seq=seq), "in_projection", batch, lt,
            [_full(h)], [c["cos"], c["sin"]],
            [norm_mix_pre[l][None], _w_in_rows(w_in_t, l, "sb"), _w_in_rows(w_in_t, l, "gla_q", "gla_r"),
             _w_in_rows(w_in_t, l, "ml_qk"), _w_in_rows(w_in_t, l, "ml_v"), _w_in_rows(w_in_t, l, "ml_o"),
             _w_in_rows(w_in_t, l, "ret"), _layer(w_small, l), _layer(wg_gla, l),
             gla_b_gate[l][None], b_gate[l][None]],
            [D_MODEL, U_SB, U_GLA, U_ML, U_RET], [BF16, BF16, F32, F32, F32])
        y_sb = _sb_call(u_sb, c["nuo"], batch, lt)
        o_gla, o_ml, o_ret = _recurrent_call(
            batch, lt, [u_gla, u_ml, u_ret],
            [ml_conv[l], c["tril"], c["e3"], c["bd_gla"], c["bd"], c["bdones"], c["onestack"],
             c["ret_dec"], c["ret_qdec"], c["ret_kdec"], c["ret_gam"]])
        (h,) = _row_call(
            _merge_kernel, "merge", batch, lt,
            [_full(hn), _full(h), _full(y_sb), _full(o_gla), _full(o_ml), _full(o_ret),
             (u_gla, BW, 2), (u_ml, BW, 3), (u_ret, BW, 3)], [],
            [_w_in_rows(w_in_t, l, "merge"), b_merge[l].reshape(1, N_BRANCH * D_MODEL), _layer(wb, l, True),
             _layer(wo, l, True),
             norm_mix_post[l][None],
             gla_norm[l][None], ml_norm[l][None], ret_norm[l][None], c["bdmean"]],
            [D_MODEL], [F32], acc_width=D_MODEL)
        ffn_consts = [norm_ffn_pre[l][None], _layer(wfg, l, True), _layer(wfu, l, True), _layer(wfd, l, True),
                      norm_ffn_post[l][None]]
        if l + 1 < depth:
            h = _ffn_call(h, batch, lt, ffn_consts).reshape(batch * lt, d)
        else:
            return _ffn_call(h, batch, lt, ffn_consts, final_seq=seq)
```
